```python
import math
import jax
import jax.numpy as jnp
from jax import lax
import numpy as np

D_MODEL = 2048
BATCH = 1
SEQ = 8192
DEPTH = 2

N_A = DEPTH // 2
N_B = DEPTH - N_A

MEM_LEN = 256
MEM_HEADS = 4
MEM_WIDTH = D_MODEL // 4
MEM_HEAD_DIM = MEM_WIDTH // MEM_HEADS
MAIN_WIDTH = D_MODEL - MEM_WIDTH

RWKV_HEAD_DIM = 64
RWKV_HEADS = MAIN_WIDTH // RWKV_HEAD_DIM
DECAY_LORA = max(32, int(round(1.8 * MAIN_WIDTH ** 0.5 / 32)) * 32)
AAA_LORA = max(32, int(round(1.8 * MAIN_WIDTH ** 0.5 / 32)) * 32)
GATE_LORA = max(32, int(round(0.6 * MAIN_WIDTH ** 0.8 / 32)) * 32)
SHIFT_WIDTH = 3 * MAIN_WIDTH + DECAY_LORA + AAA_LORA + GATE_LORA
A_IN_WIDTH = SHIFT_WIDTH + MEM_WIDTH
GN_EPS = 64e-5

DIFF_HEADS = 12
DIFF_HEAD_DIM = MAIN_WIDTH // DIFF_HEADS // 2
ROPE_DIM = DIFF_HEAD_DIM // 4
ROPE_THETA = 500000.0
Q_BLOCK = 128
B_IN_WIDTH = MAIN_WIDTH + MEM_WIDTH

N_EXPERTS = 64
TOP_K = 8
N_GROUPS = 8
TOPK_GROUPS = 4
EXPERT_DIM = D_MODEL // 4
SHARED_DIM = D_MODEL // 4
ROUTED_SCALE = 2.5
MOE_BLOCK = 128

ALPHA = (2 * DEPTH) ** 0.25
BETA = (8 * DEPTH) ** -0.25
LN_EPS = 1e-5
RMS_EPS = 1e-5

kernel_name = 'yoco_rwkv7_diffattn_moe_trunk'


def _split(z, sizes):
    idx = np.cumsum(sizes)[:-1].tolist()
    return jnp.split(z, idx, axis=-1)


def layer_norm(x, g, b):
    xf = x.astype(jnp.float32)
    mu = jnp.mean(xf, -1, keepdims=True)
    var = jnp.mean(jnp.square(xf - mu), -1, keepdims=True)
    return ((xf - mu) * lax.rsqrt(var + LN_EPS)).astype(x.dtype) * g + b


def token_shift(z):
    return jnp.pad(z, ((0, 0), (1, 0), (0, 0)))[:, :-1]


def rope_partial(u, positions):
    half = ROPE_DIM // 2
    inv_freq = ROPE_THETA ** (-jnp.arange(half, dtype=jnp.float32) / half)
    ang = positions.astype(jnp.float32)[..., None] * inv_freq
    cos = jnp.cos(ang)[:, :, None, :]
    sin = jnp.sin(ang)[:, :, None, :]
    u1 = u[..., :half].astype(jnp.float32)
    u2 = u[..., half:ROPE_DIM].astype(jnp.float32)
    rot = jnp.concatenate([u1 * cos - u2 * sin, u2 * cos + u1 * sin], -1).astype(u.dtype)
    return jnp.concatenate([rot, u[..., ROPE_DIM:]], -1)


def wkv7_scan(r, decay, k, v, a, b):
    bsz, t, h, n = r.shape

    def step(S, inp):
        r_t, w_t, k_t, v_t, a_t, b_t = inp
        sa = jnp.einsum('bhij,bhj->bhi', S, a_t)
        S = (S * w_t[:, :, None, :] + sa[..., None] * b_t[:, :, None, :]
             + v_t[..., None] * k_t[:, :, None, :])
        return S, jnp.einsum('bhij,bhj->bhi', S, r_t)

    xs = tuple(jnp.moveaxis(u.astype(jnp.float32), 1, 0) for u in (r, decay, k, v, a, b))
    S0 = jnp.zeros((bsz, h, n, n), jnp.float32)
    _, ys = lax.scan(step, S0, xs)
    return jnp.moveaxis(ys, 0, 1)


def rwkv7_time_mix(zs, w0, w_decay_up, a0, w_a_up, w_g_up, k_k, k_a, r_k, gn_g, gn_b):
    bsz, t, _ = zs.shape
    r, k, v, dw, da, dg = _split(zs, (MAIN_WIDTH, MAIN_WIDTH, MAIN_WIDTH, DECAY_LORA, AAA_LORA, GATE_LORA))
    heads = lambda u: u.reshape(bsz, t, RWKV_HEADS, RWKV_HEAD_DIM)
    w_log = -jax.nn.softplus(-(w0 + jnp.tanh(dw) @ w_decay_up).astype(jnp.float32)) - 0.5
    decay = jnp.exp(-jnp.exp(w_log))
    a = jax.nn.sigmoid(a0 + da @ w_a_up)
    g = jax.nn.sigmoid(dg) @ w_g_up
    kk = heads(k * k_k).astype(jnp.float32)
    kk = kk / jnp.maximum(jnp.sqrt(jnp.sum(jnp.square(kk), -1, keepdims=True)), 1e-12)
    k = k * (1.0 + (a - 1.0) * k_a)
    rh, kh, vh = heads(r), heads(k), heads(v)
    y = wkv7_scan(rh, heads(decay), kh, vh, -kk, kk * heads(a).astype(jnp.float32))
    mu = jnp.mean(y, -1, keepdims=True)
    var = jnp.mean(jnp.square(y - mu), -1, keepdims=True)
    y = ((y - mu) * lax.rsqrt(var + GN_EPS)).reshape(bsz, t, MAIN_WIDTH).astype(zs.dtype) * gn_g + gn_b
    bonus = (jnp.sum(rh * kh * r_k, -1, keepdims=True) * vh).reshape(bsz, t, MAIN_WIDTH)
    return (y + bonus) * g


def memory_attention(q, mk, mv):
    bsz, t, _ = q.shape
    qh = q.reshape(bsz, t, MEM_HEADS, MEM_HEAD_DIM)
    kh = mk.reshape(bsz, -1, MEM_HEADS, MEM_HEAD_DIM)
    vh = mv.reshape(bsz, -1, MEM_HEADS, MEM_HEAD_DIM)
    s = jnp.einsum('bthd,bmhd->bhtm', qh, kh).astype(jnp.float32) * (MEM_HEAD_DIM ** -0.5)
    p = jax.nn.softmax(s, axis=-1)
    o = jnp.einsum('bhtm,bmhd->bthd', p.astype(vh.dtype), vh)
    return o.reshape(bsz, t, MEM_WIDTH)


def diff_attention(q, k, v, lam):
    bsz, t, h2, d = q.shape
    nb = t // Q_BLOCK
    qb = jnp.moveaxis(q.reshape(bsz, nb, Q_BLOCK, h2, d), 1, 0)
    kpos = jnp.arange(t)

    def one_block(args):
        qi, bi = args
        s = jnp.einsum('bqhd,bkhd->bhqk', qi, k).astype(jnp.float32) * (d ** -0.5)
        qpos = bi * Q_BLOCK + jnp.arange(Q_BLOCK)
        s = jnp.where(kpos[None, :] <= qpos[:, None], s, -jnp.inf)
        p = jax.nn.softmax(s, axis=-1).reshape(bsz, DIFF_HEADS, 2, Q_BLOCK, t)
        pd = p[:, :, 0] - lam * p[:, :, 1]
        return jnp.einsum('bhqk,bkhe->bqhe', pd.astype(v.dtype), v)

    o = lax.map(one_block, (qb, jnp.arange(nb)))
    return jnp.moveaxis(o, 0, 1).reshape(bsz, t, DIFF_HEADS, 2 * d)


def moe_ffn(x, w_router, e_bias, w1, w3, w2, ws1, ws3, ws2):
    bsz, t, dm = x.shape
    xt = x.reshape(-1, dm)
    n = xt.shape[0]
    scores = jax.nn.sigmoid((xt @ w_router).astype(jnp.float32))
    biased = scores + e_bias.astype(jnp.float32)
    grp = biased.reshape(n, N_GROUPS, N_EXPERTS // N_GROUPS)
    grp_score = jnp.sum(lax.top_k(grp, 2)[0], -1)
    _, gidx = lax.top_k(grp_score, TOPK_GROUPS)
    gmask = jnp.any(gidx[..., None] == jnp.arange(N_GROUPS), axis=1)
    emask = jnp.repeat(gmask, N_EXPERTS // N_GROUPS, axis=1)
    _, eidx = lax.top_k(jnp.where(emask, biased, -jnp.inf), TOP_K)
    wts = jnp.take_along_axis(scores, eidx, axis=1)
    wts = wts / (jnp.sum(wts, -1, keepdims=True) + 1e-20) * ROUTED_SCALE

    nk = n * TOP_K
    flat_e = eidx.reshape(-1)
    flat_tok = jnp.arange(nk, dtype=jnp.int32) // TOP_K
    flat_w = wts.reshape(-1)
    order = jnp.argsort(flat_e)
    se = flat_e[order]
    counts = jnp.zeros((N_EXPERTS,), jnp.int32).at[flat_e].add(1)
    starts = jnp.cumsum(counts) - counts
    pcounts = (counts + MOE_BLOCK - 1) // MOE_BLOCK * MOE_BLOCK
    pends = jnp.cumsum(pcounts)
    pstarts = pends - pcounts
    dest = pstarts[se] + jnp.arange(nk, dtype=jnp.int32) - starts[se]
    n_blocks = (nk + MOE_BLOCK - 1) // MOE_BLOCK + N_EXPERTS
    cap = n_blocks * MOE_BLOCK
    buf_tok = jnp.full((cap,), n, jnp.int32).at[dest].set(flat_tok[order])
    buf_w = jnp.zeros((cap,), jnp.float32).at[dest].set(flat_w[order])
    blk_start = jnp.arange(n_blocks, dtype=jnp.int32) * MOE_BLOCK
    blk_e = jnp.minimum(jnp.sum(blk_start[:, None] >= pends[None, :], axis=1), N_EXPERTS - 1)
    x_pad = jnp.concatenate([xt, jnp.zeros((1, dm), xt.dtype)], 0)

    def body(acc, inp):
        tok, w, e = inp
        xb = x_pad[tok]
        hb = jax.nn.silu(xb @ w1[e]) * (xb @ w3[e])
        yb = (hb @ w2[e]) * w[:, None].astype(xb.dtype)
        return acc.at[tok].add(yb), None

    acc0 = jnp.zeros((n + 1, dm), xt.dtype)
    acc, _ = lax.scan(body, acc0, (buf_tok.reshape(n_blocks, MOE_BLOCK),
                                   buf_w.reshape(n_blocks, MOE_BLOCK), blk_e))
    shared = (jax.nn.silu(xt @ ws1) * (xt @ ws3)) @ ws2
    return (acc[:n] + shared).reshape(bsz, t, dm)


def setup_inputs(seed: int = 0) -> dict:
    key = jax.random.key(seed)
    ks = iter(jax.random.split(key, 64))
    nrm = lambda shape, scale: jax.random.normal(next(ks), shape, jnp.float32) * scale
    uni = lambda shape, lo, hi: jax.random.uniform(next(ks), shape, jnp.float32, lo, hi)
    D, C = D_MODEL, MAIN_WIDTH
    x = nrm((BATCH, SEQ, D), 1.0)
    mem = nrm((BATCH, MEM_LEN, D), 1.0)
    positions = (jnp.arange(SEQ, dtype=jnp.int32)[None, :]
                 + jax.random.randint(next(ks), (BATCH, 1), 0, 1024, dtype=jnp.int32))
    a_w_in = nrm((N_A, D, A_IN_WIDTH), D ** -0.5).at[:, :, 2 * C:3 * C].multiply(BETA)
    a_mu = uni((N_A, SHIFT_WIDTH), 0.0, 1.0)
    a_w0 = uni((N_A, C), -5.5, 0.5)
    a_w_decay_up = nrm((N_A, DECAY_LORA, C), 0.5 * DECAY_LORA ** -0.5)
    a_a0 = nrm((N_A, C), 0.1)
    a_w_a_up = nrm((N_A, AAA_LORA, C), 0.5 * AAA_LORA ** -0.5)
    a_w_g_up = nrm((N_A, GATE_LORA, C), GATE_LORA ** -0.5)
    a_k_k = 0.85 + nrm((N_A, C), 0.05)
    a_k_a = 1.0 + nrm((N_A, C), 0.05)
    a_r_k = nrm((N_A, RWKV_HEADS, RWKV_HEAD_DIM), 0.1)
    a_gn_g = 1.0 + nrm((N_A, C), 0.05)
    a_gn_b = nrm((N_A, C), 0.01)
    kv_shared_w = nrm((D, 2 * C), D ** -0.5).at[:, C:].multiply(BETA)
    b_w_in = nrm((N_B, D, B_IN_WIDTH), D ** -0.5)
    b_lambda_q1 = nrm((N_B, DIFF_HEAD_DIM), 0.1)
    b_lambda_k1 = nrm((N_B, DIFF_HEAD_DIM), 0.1)
    b_lambda_q2 = nrm((N_B, DIFF_HEAD_DIM), 0.1)
    b_lambda_k2 = nrm((N_B, DIFF_HEAD_DIM), 0.1)
    b_subln_g = 1.0 + nrm((N_B, 2 * DIFF_HEAD_DIM), 0.05)
    mem_w_kv = nrm((DEPTH, D, 2 * MEM_WIDTH), D ** -0.5).at[:, :, MEM_WIDTH:].multiply(BETA)
    w_out = nrm((DEPTH, D, D), BETA * D ** -0.5)
    ln1_g = 1.0 + nrm((DEPTH, D), 0.05)
    ln1_b = nrm((DEPTH, D), 0.01)
    ln2_g = 1.0 + nrm((DEPTH, D), 0.05)
    ln2_b = nrm((DEPTH, D), 0.01)
    router_w = nrm((DEPTH, D, N_EXPERTS), D ** -0.5)
    router_bias = nrm((DEPTH, N_EXPERTS), 0.01)
    exp_w1 = nrm((DEPTH, N_EXPERTS, D, EXPERT_DIM), D ** -0.5)
    exp_w3 = nrm((DEPTH, N_EXPERTS, D, EXPERT_DIM), D ** -0.5)
    exp_w2 = nrm((DEPTH, N_EXPERTS, EXPERT_DIM, D), BETA * EXPERT_DIM ** -0.5)
    sh_w1 = nrm((DEPTH, D, SHARED_DIM), D ** -0.5)
    sh_w3 = nrm((DEPTH, D, SHARED_DIM), D ** -0.5)
    sh_w2 = nrm((DEPTH, SHARED_DIM, D), BETA * SHARED_DIM ** -0.5)
    return {'x': x, 'mem': mem, 'positions': positions,
            'a_w_in': a_w_in, 'a_mu': a_mu, 'a_w0': a_w0, 'a_w_decay_up': a_w_decay_up,
            'a_a0': a_a0, 'a_w_a_up': a_w_a_up, 'a_w_g_up': a_w_g_up, 'a_k_k': a_k_k,
            'a_k_a': a_k_a, 'a_r_k': a_r_k, 'a_gn_g': a_gn_g, 'a_gn_b': a_gn_b,
            'kv_shared_w': kv_shared_w, 'b_w_in': b_w_in, 'b_lambda_q1': b_lambda_q1,
            'b_lambda_k1': b_lambda_k1, 'b_lambda_q2': b_lambda_q2, 'b_lambda_k2': b_lambda_k2,
            'b_subln_g': b_subln_g, 'mem_w_kv': mem_w_kv, 'w_out': w_out,
            'ln1_g': ln1_g, 'ln1_b': ln1_b, 'ln2_g': ln2_g, 'ln2_b': ln2_b,
            'router_w': router_w, 'router_bias': router_bias,
            'exp_w1': exp_w1, 'exp_w3': exp_w3, 'exp_w2': exp_w2,
            'sh_w1': sh_w1, 'sh_w3': sh_w3, 'sh_w2': sh_w2}


def reference(x, mem, positions, a_w_in, a_mu, a_w0, a_w_decay_up, a_a0, a_w_a_up, a_w_g_up,
              a_k_k, a_k_a, a_r_k, a_gn_g, a_gn_b, kv_shared_w, b_w_in, b_lambda_q1,
              b_lambda_k1, b_lambda_q2, b_lambda_k2, b_subln_g, mem_w_kv, w_out,
              ln1_g, ln1_b, ln2_g, ln2_b, router_w, router_bias, exp_w1, exp_w3, exp_w2,
              sh_w1, sh_w3, sh_w2):
    bsz, t, _ = x.shape
    h = x
    k_s = None
    v_s = None
    for l in range(DEPTH):
        mk, mv = _split(mem @ mem_w_kv[l], (MEM_WIDTH, MEM_WIDTH))
        if l < N_A:
            i = l
            z = h @ a_w_in[i]
            zs, q_mem = _split(z, (SHIFT_WIDTH, MEM_WIDTH))
            zs = zs + (token_shift(zs) - zs) * a_mu[i]
            main = rwkv7_time_mix(zs, a_w0[i], a_w_decay_up[i], a_a0[i], a_w_a_up[i],
                                  a_w_g_up[i], a_k_k[i], a_k_a[i], a_r_k[i],
                                  a_gn_g[i], a_gn_b[i])
        else:
            j = l - N_A
            if j == 0:
                ks_, vs_ = _split(h @ kv_shared_w, (MAIN_WIDTH, MAIN_WIDTH))
                k_s = rope_partial(ks_.reshape(bsz, t, 2 * DIFF_HEADS, DIFF_HEAD_DIM), positions)
                v_s = vs_.reshape(bsz, t, DIFF_HEADS, 2 * DIFF_HEAD_DIM)
            z = h @ b_w_in[j]
            q, q_mem = _split(z, (MAIN_WIDTH, MEM_WIDTH))
            q = rope_partial(q.reshape(bsz, t, 2 * DIFF_HEADS, DIFF_HEAD_DIM), positions)
            lam_init = 0.8 - 0.6 * math.exp(-0.3 * l)
            lam = (jnp.exp(jnp.sum(b_lambda_q1[j] * b_lambda_k1[j]).astype(jnp.float32))
                   - jnp.exp(jnp.sum(b_lambda_q2[j] * b_lambda_k2[j]).astype(jnp.float32))
                   + lam_init)
            o = diff_attention(q, k_s, v_s, lam)
            of = o.astype(jnp.float32)
            o = (of * lax.rsqrt(jnp.mean(jnp.square(of), -1, keepdims=True) + RMS_EPS)).astype(h.dtype) * b_subln_g[j]
            main = (o * (1.0 - lam_init)).reshape(bsz, t, MAIN_WIDTH)
        memo = memory_attention(q_mem, mk, mv)
        mix = jnp.concatenate([main, memo], -1) @ w_out[l]
        h = layer_norm(ALPHA * h + mix, ln1_g[l], ln1_b[l])
        ffn = moe_ffn(h, router_w[l], router_bias[l], exp_w1[l], exp_w3[l], exp_w2[l],
                      sh_w1[l], sh_w3[l], sh_w2[l])
        h = layer_norm(ALPHA * h + ffn, ln2_g[l], ln2_b[l])
    return h
```

```python
import functools
import math

import jax
import jax.numpy as jnp
from jax import lax
from jax.experimental import pallas as pl
from jax.experimental.pallas import tpu as pltpu

F32 = jnp.float32
BF16 = jnp.bfloat16

D_MODEL = 2048
DEPTH = 2
MEM_HEADS = 4
MEM_WIDTH = D_MODEL // 4
MEM_HEAD_DIM = MEM_WIDTH // MEM_HEADS
MAIN_WIDTH = D_MODEL - MEM_WIDTH
RWKV_HEAD_DIM = 64
DECAY_LORA = 64
AAA_LORA = 64
GATE_LORA = 224
GN_EPS = 64e-5
DIFF_HEADS = 12
DIFF_HEAD_DIM = 64
ROPE_DIM = DIFF_HEAD_DIM // 4
ROPE_THETA = 500000.0
N_EXPERTS = 64
TOP_K = 8
N_GROUPS = 8
GROUP_SIZE = N_EXPERTS // N_GROUPS
TOPK_GROUPS = 4
EXPERT_DIM = D_MODEL // 4
ROUTED_SCALE = 2.5
ALPHA = (2 * DEPTH) ** 0.25
LN_EPS = 1e-5
RMS_EPS = 1e-5

LANES = 128
VMEM_LIMIT = 56 * 1024 * 1024

LORA_TILE = 512
A_IN_PAD = 3 * MAIN_WIDTH + LORA_TILE + MEM_WIDTH
WKV_CHUNK = 64
EXPERT_ROWS = 256


def _params(sem):
    return pltpu.CompilerParams(dimension_semantics=sem, vmem_limit_bytes=VMEM_LIMIT)


def _split2(x):
    hi = x.astype(BF16)
    lo = (x - hi.astype(F32)).astype(BF16)
    return hi, lo


def _dg(a, b, dims):
    return lax.dot_general(a, b, (dims, ((), ())), preferred_element_type=F32)


_NN = ((1,), (0,))
_NT = ((1,), (1,))
_TN = ((0,), (0,))


def _dot3(a, b, dims=_NN):
    ah, al = _split2(a)
    bh, bl = _split2(b)
    return _dg(ah, bh, dims) + (_dg(ah, bl, dims) + _dg(al, bh, dims))


def _dot2l(a, b_exact, dims=_NN):
    ah, al = _split2(a)
    return _dg(ah, b_exact, dims) + _dg(al, b_exact, dims)


def _dot1(a, b, dims=_NN):
    return _dg(a.astype(BF16), b.astype(BF16), dims)


def _sigmoid(x):
    return 1.0 / (1.0 + jnp.exp(-x))


def _proj_kernel(*refs, n_rope, tn, cast_x):
    if n_rope:
        x_ref, w_ref, c_ref, sa_ref, sb_ref, o_ref = refs[:6]
        rest = refs[6:]
    else:
        x_ref, w_ref, o_ref = refs[:3]
        rest = refs[3:]
    j = pl.program_id(1)
    if cast_x:
        xb_ref, = rest

        @pl.when(j == 0)
        def _():
            xb_ref[...] = x_ref[...].astype(BF16)
        xb = xb_ref[...]
    else:
        xb = x_ref[...]
    acc = jnp.dot(xb, w_ref[...], preferred_element_type=F32)
    if n_rope:
        @pl.when(j < n_rope)
        def _():
            c, sa, sb = c_ref[...], sa_ref[...], sb_ref[...]
            for q in range(tn // LANES):
                a = acc[:, q * LANES:(q + 1) * LANES]
                rot = a * c + pltpu.roll(a, LANES - ROPE_DIM // 2, 1) * sa + pltpu.roll(a, ROPE_DIM // 2, 1) * sb
                o_ref[:, q * LANES:(q + 1) * LANES] = rot.astype(o_ref.dtype)

        @pl.when(j >= n_rope)
        def _():
            o_ref[...] = acc.astype(o_ref.dtype)
    else:
        o_ref[...] = acc.astype(o_ref.dtype)


def _proj(x, w, *, tm, tn, out_dtype, rope=None, n_rope=0):
    m, k = x.shape
    n = w.shape[1]
    assert m % tm == 0 and n % tn == 0
    cast_x = x.dtype != BF16
    in_specs = [pl.BlockSpec((tm, k), lambda i, j: (i, 0)),
                pl.BlockSpec((k, tn), lambda i, j: (0, j))]
    args = [x, w]
    if n_rope:
        in_specs += [pl.BlockSpec((tm, LANES), lambda i, j: (i, 0))] * 3
        args += list(rope)
    return pl.pallas_call(
        functools.partial(_proj_kernel, n_rope=n_rope, tn=tn, cast_x=cast_x),
        grid=(m // tm, n // tn),
        in_specs=in_specs,
        out_specs=pl.BlockSpec((tm, tn), lambda i, j: (i, j)),
        out_shape=jax.ShapeDtypeStruct((m, n), out_dtype),
        scratch_shapes=[pltpu.VMEM((tm, k), BF16)] if cast_x else [],
        compiler_params=_params(("parallel", "arbitrary")),
        name="proj",
    )(*args)


def _rope_table_kernel(pos_ref, invf_ref, c_ref, sa_ref, sb_ref):
    ang = pos_ref[...].astype(F32) * invf_ref[...]
    lane = lax.broadcasted_iota(jnp.int32, ang.shape, 1) & (DIFF_HEAD_DIM - 1)
    c = jnp.cos(ang)
    s = jnp.sin(ang)
    half = ROPE_DIM // 2
    c_ref[...] = jnp.where(lane < ROPE_DIM, c, 1.0)
    sa_ref[...] = jnp.where(lane < half, -s, 0.0)
    sb_ref[...] = jnp.where(lane < half, 0.0, jnp.where(lane < ROPE_DIM, s, 0.0))


def _rope_tables(positions, tm):
    t = positions.shape[1]
    half = ROPE_DIM // 2
    inv_freq = ROPE_THETA ** (-jnp.arange(half, dtype=F32) / half)
    lane = jnp.arange(LANES) % DIFF_HEAD_DIM
    invf = jnp.where(lane < ROPE_DIM, inv_freq[lane % half], 0.0).reshape(1, LANES).astype(F32)
    pos = positions.reshape(t, 1)
    shp = jax.ShapeDtypeStruct((t, LANES), F32)
    return pl.pallas_call(
        _rope_table_kernel,
        grid=(t // tm,),
        in_specs=[pl.BlockSpec((tm, 1), lambda i: (i, 0)), pl.BlockSpec((1, LANES), lambda i: (0, 0))],
        out_specs=[pl.BlockSpec((tm, LANES), lambda i: (i, 0))] * 3,
        out_shape=[shp, shp, shp],
        compiler_params=_params(("parallel",)),
        name="rope_tables",
    )(pos, invf)


def _head_sum(x, g128):
    return _dot2l(x, g128)


def _prep_kernel(z_ref, mu_ref, w0_ref, wdu_ref, a0_ref, wau_ref, wgu_ref, kk_ref, ka_ref, g128_ref,
                 r_o, lw_o, k_o, v_o, an_o, b_o, g_o, carry_ref, *, tm):
    i = pl.program_id(0)

    @pl.when(i == 0)
    def _():
        carry_ref[...] = jnp.zeros_like(carry_ref)

    row0 = lax.broadcasted_iota(jnp.int32, (tm, 1), 0) == 0
    c = MAIN_WIDTH

    def shifted(c0, c1):
        z = z_ref[:, c0:c1]
        prev = jnp.where(row0, carry_ref[:, c0:c1], pltpu.roll(z, 1, 0))
        return z + (prev - z) * mu_ref[:, c0:c1]

    r = shifted(0, c)
    k = shifted(c, 2 * c)
    v = shifted(2 * c, 3 * c)
    lora = shifted(3 * c, 3 * c + LORA_TILE)
    carry_ref[...] = z_ref[pl.ds(tm - 1, 1), :]

    dw = lora[:, 0:LANES]
    da = lora[:, LANES:2 * LANES]
    dg = lora[:, 2 * LANES:4 * LANES]
    wl = w0_ref[...] + _dot3(jnp.tanh(dw), wdu_ref[...])
    lw = -_sigmoid(wl) * math.exp(-0.5)
    a = _sigmoid(a0_ref[...] + _dot3(da, wau_ref[...]))
    g = _dot3(_sigmoid(dg), wgu_ref[...])
    kk = k * kk_ref[...]
    g128 = g128_ref[...]
    for q in range(c // LANES):
        sl = slice(q * LANES, (q + 1) * LANES)
        kq = kk[:, sl]
        nrm = jnp.maximum(jnp.sqrt(_head_sum(kq * kq, g128)), 1e-12)
        kn = kq / nrm
        an_o[:, sl] = -kn
        b_o[:, sl] = kn * a[:, sl]
    r_o[...] = r
    lw_o[...] = lw
    k_o[...] = k * (1.0 + (a - 1.0) * ka_ref[...])
    v_o[...] = v
    g_o[...] = g


def _rwkv_prep(z, mu_pad, w0, wdu_pad, a0, wau_pad, wgu_pad, k_k, k_a, g128, tm):
    t = z.shape[0]
    c = MAIN_WIDTH
    wz = 3 * c + LORA_TILE
    row = lambda n: pl.BlockSpec((1, n), lambda i: (0, 0))
    full = lambda a: pl.BlockSpec(a.shape, lambda i: (0, 0))
    shp = jax.ShapeDtypeStruct((t, c), F32)
    return pl.pallas_call(
        functools.partial(_prep_kernel, tm=tm),
        grid=(t // tm,),
        in_specs=[pl.BlockSpec((tm, wz), lambda i: (i, 0)), row(wz), row(c), full(wdu_pad), row(c),
                  full(wau_pad), full(wgu_pad), row(c), row(c), full(g128)],
        out_specs=[pl.BlockSpec((tm, c), lambda i: (i, 0))] * 7,
        out_shape=[shp] * 7,
        scratch_shapes=[pltpu.VMEM((1, wz), F32)],
        compiler_params=_params(("arbitrary",)),
        name="rwkv_prep",
    )(z, mu_pad, w0, wdu_pad, a0, wau_pad, wgu_pad, k_k, k_a, g128)


def _wkv_kernel(r_ref, lw_ref, k_ref, v_ref, an_ref, b_ref, g_ref, rk_ref, gng_ref, gnb_ref, g128_ref,
                o_ref, ht_ref, *, L):
    ci = pl.program_id(1)

    @pl.when(ci == 0)
    def _():
        ht_ref[...] = jnp.zeros_like(ht_ref)

    r, lw, k, v = r_ref[...], lw_ref[...], k_ref[...], v_ref[...]
    an, b = an_ref[...], b_ref[...]
    n2 = 2 * L

    rowi = lax.broadcasted_iota(jnp.int32, (L, L), 0)
    coli = lax.broadcasted_iota(jnp.int32, (L, L), 1)
    tri = (coli <= rowi).astype(BF16)
    lw_h = lw.astype(BF16)
    lw_r = lw - lw_h.astype(F32)
    lw_m = lw_r.astype(BF16)
    lw_l = (lw_r - lw_m.astype(F32)).astype(BF16)
    cum = _dg(tri, lw_h, _NN) + (_dg(tri, lw_m, _NN) + _dg(tri, lw_l, _NN))
    cum_last = cum[L - 1:L, :]
    p_in = jnp.exp(cum)
    p_inv = jnp.exp(-cum)
    at = an * jnp.exp(cum - lw)
    rt = r * p_in
    bt = b * p_inv
    kt = k * p_inv
    p_end = jnp.exp(cum_last - cum)

    head0 = lax.broadcasted_iota(jnp.int32, (L, LANES), 1) < RWKV_HEAD_DIM

    def stack(x):
        return jnp.concatenate([jnp.where(head0, x, 0.0), jnp.where(head0, 0.0, x)], axis=0)

    ar = jnp.concatenate([stack(at), stack(rt)], axis=0)
    bk = jnp.concatenate([stack(bt), stack(kt)], axis=0)
    v_st = stack(v)
    m = _dot3(ar, bk, _NT)

    ri = lax.broadcasted_iota(jnp.int32, (n2, n2), 0)
    cj = lax.broadcasted_iota(jnp.int32, (n2, n2), 1)
    same = (ri >= L) == (cj >= L)
    strict = same & (cj < ri)
    incl = same & (cj <= ri)
    ab = jnp.where(strict, m[:n2, :n2], 0.0)
    ak = jnp.where(strict, m[:n2, n2:], 0.0)
    rb = jnp.where(incl, m[n2:, :n2], 0.0)
    rkm = jnp.where(incl, m[n2:, n2:], 0.0)

    ht = ht_ref[...]
    arh = _dot3(ar, ht, _NT)
    kv = _dot3(jnp.concatenate([ak, rkm], axis=0), v_st)
    u = arh[:n2] + kv[:n2]
    apow = ab
    steps = max(1, (L - 1).bit_length())
    for s in range(steps):
        u = u + _dot3(apow, u)
        if s + 1 < steps:
            apow = _dot3(apow, apow)
    y_st = arh[n2:] + kv[n2:] + _dot3(rb, u)
    y = y_st[:L] + y_st[L:]

    bk_end = jnp.concatenate([stack(b * p_end), stack(k * p_end)], axis=0)
    uv = jnp.concatenate([u, v_st], axis=0)
    ht_ref[...] = ht * jnp.exp(cum_last) + _dot3(uv, bk_end, _TN)

    g128 = g128_ref[...]
    inv_n = 1.0 / RWKV_HEAD_DIM
    mu = _head_sum(y, g128) * inv_n
    yc = y - mu
    var = _head_sum(yc * yc, g128) * inv_n
    yn = yc * lax.rsqrt(var + GN_EPS) * gng_ref[...] + gnb_ref[...]
    bonus = _head_sum(r * k * rk_ref[...], g128) * v
    o_ref[...] = (yn + bonus) * g_ref[...]


def _wkv(r, lw, k, v, an, b, g, r_k, gn_g, gn_b, g128):
    t, c = r.shape
    L = WKV_CHUNK
    tok = pl.BlockSpec((L, LANES), lambda p, ci: (ci, p))
    par = pl.BlockSpec((1, LANES), lambda p, ci: (0, p))
    return pl.pallas_call(
        functools.partial(_wkv_kernel, L=L),
        grid=(c // LANES, t // L),
        in_specs=[tok] * 7 + [par] * 3 + [pl.BlockSpec((LANES, LANES), lambda p, ci: (0, 0))],
        out_specs=tok,
        out_shape=jax.ShapeDtypeStruct((t, c), F32),
        scratch_shapes=[pltpu.VMEM((LANES, LANES), F32)],
        compiler_params=_params(("parallel", "arbitrary")),
        name="wkv7",
    )(r, lw, k, v, an, b, g, r_k, gn_g, gn_b, g128)


def _memattn_kernel(q_ref, kv_ref, o_ref):
    scale = MEM_HEAD_DIM ** -0.5
    for hh in range(MEM_HEADS):
        sl = slice(hh * MEM_HEAD_DIM, (hh + 1) * MEM_HEAD_DIM)
        q = q_ref[:, sl].astype(BF16)
        kh = kv_ref[:, sl].astype(BF16)
        vh = kv_ref[:, MEM_WIDTH + hh * MEM_HEAD_DIM:MEM_WIDTH + (hh + 1) * MEM_HEAD_DIM].astype(BF16)
        s = _dg(q, kh, _NT) * scale
        s = s - jnp.max(s, axis=-1, keepdims=True)
        e = jnp.exp(s)
        p = e / jnp.sum(e, axis=-1, keepdims=True)
        o_ref[:, sl] = _dg(p.astype(BF16), vh, _NN)


def _memattn(zq, q_col_block, memkv, tm):
    t = zq.shape[0]
    return pl.pallas_call(
        _memattn_kernel,
        grid=(t // tm,),
        in_specs=[pl.BlockSpec((tm, MEM_WIDTH), lambda i: (i, q_col_block)),
                  pl.BlockSpec(memkv.shape, lambda i: (0, 0))],
        out_specs=pl.BlockSpec((tm, MEM_WIDTH), lambda i: (i, 0)),
        out_shape=jax.ShapeDtypeStruct((t, MEM_WIDTH), F32),
        compiler_params=_params(("parallel",)),
        name="mem_attn",
    )(zq, memkv)


def _layer_norm(x, g, b):
    mu = jnp.mean(x, axis=-1, keepdims=True)
    xc = x - mu
    var = jnp.mean(xc * xc, axis=-1, keepdims=True)
    return xc * lax.rsqrt(var + LN_EPS) * g + b


def _outproj_kernel(main_ref, memo_ref, h_ref, w_ref, g_ref, b_ref, o_ref):
    mix = (_dg(main_ref[...].astype(BF16), w_ref[:MAIN_WIDTH, :], _NN)
           + _dg(memo_ref[...].astype(BF16), w_ref[MAIN_WIDTH:, :], _NN))
    o_ref[...] = _layer_norm(ALPHA * h_ref[...] + mix, g_ref[...], b_ref[...])


def _outproj_ln(main, memo, h, w_bf, g, b, tm):
    t = h.shape[0]
    row = pl.BlockSpec((1, D_MODEL), lambda i: (0, 0))
    return pl.pallas_call(
        _outproj_kernel,
        grid=(t // tm,),
        in_specs=[pl.BlockSpec((tm, MAIN_WIDTH), lambda i: (i, 0)),
                  pl.BlockSpec((tm, MEM_WIDTH), lambda i: (i, 0)),
                  pl.BlockSpec((tm, D_MODEL), lambda i: (i, 0)),
                  pl.BlockSpec((D_MODEL, D_MODEL), lambda i: (0, 0)), row, row],
        out_specs=pl.BlockSpec((tm, D_MODEL), lambda i: (i, 0)),
        out_shape=jax.ShapeDtypeStruct((t, D_MODEL), F32),
        compiler_params=_params(("parallel",)),
        name="outproj_ln",
    )(main, memo, h, w_bf, g, b)


def _first_max(vals, ids):
    m = vals[0]
    for x in vals[1:]:
        m = jnp.maximum(m, x)
    m = jnp.max(m, axis=0, keepdims=True)
    big = jnp.int32(1 << 20)
    idx = None
    for x, e in zip(vals, ids):
        cand = jnp.where(x == m, e, big)
        idx = cand if idx is None else jnp.minimum(idx, cand)
    idx = jnp.min(idx, axis=0, keepdims=True)
    return m, idx


def _router_kernel(h_ref, wt_ref, bias_ref, eidx_ref, pos_ref, wts_ref, cnt_ref, run_ref, *, tm):
    i = pl.program_id(0)

    @pl.when(i == 0)
    def _():
        run_ref[...] = jnp.zeros_like(run_ref)

    neg = -jnp.inf
    logits = _dot3(wt_ref[...], h_ref[...], _NT)
    scores = _sigmoid(logits)
    biased = scores + bias_ref[...]
    sub = lax.broadcasted_iota(jnp.int32, (GROUP_SIZE, tm), 0)
    grp = [biased[g * GROUP_SIZE:(g + 1) * GROUP_SIZE, :] for g in range(N_GROUPS)]
    sc = [scores[g * GROUP_SIZE:(g + 1) * GROUP_SIZE, :] for g in range(N_GROUPS)]
    ids = [sub + g * GROUP_SIZE for g in range(N_GROUPS)]

    gs = []
    for g in range(N_GROUPS):
        m1, i1 = _first_max([grp[g]], [sub])
        m2 = jnp.max(jnp.where(sub == i1, neg, grp[g]), axis=0, keepdims=True)
        gs.append(m1 + m2)
    gscore = jnp.concatenate(gs, axis=0)
    gmask = jnp.zeros((N_GROUPS, tm), jnp.bool_)
    cur = gscore
    for _ in range(TOPK_GROUPS):
        _, gi = _first_max([cur], [sub])
        hit = sub == gi
        gmask = gmask | hit
        cur = jnp.where(hit, neg, cur)
    gm = gmask.astype(F32)
    cur = [jnp.where(gm[g:g + 1, :] > 0.0, grp[g], neg) for g in range(N_GROUPS)]

    sel = [jnp.zeros((GROUP_SIZE, tm), jnp.bool_) for _ in range(N_GROUPS)]
    e_rows, w_rows = [], []
    for _ in range(TOP_K):
        _, ei = _first_max(cur, ids)
        w = jnp.zeros((1, tm), F32)
        for g in range(N_GROUPS):
            hit = ids[g] == ei
            sel[g] = sel[g] | hit
            cur[g] = jnp.where(hit, neg, cur[g])
            w = w + jnp.sum(jnp.where(hit, sc[g], 0.0), axis=0, keepdims=True)
        e_rows.append(ei)
        w_rows.append(w)
    eidx = jnp.concatenate(e_rows, axis=0)
    wraw = jnp.concatenate(w_rows, axis=0)
    wts_ref[...] = wraw / (jnp.sum(wraw, axis=0, keepdims=True) + 1e-20) * ROUTED_SCALE
    eidx_ref[...] = eidx

    self32 = jnp.concatenate([s.astype(F32) for s in sel], axis=0)
    ui = lax.broadcasted_iota(jnp.int32, (tm, tm), 0)
    uj = lax.broadcasted_iota(jnp.int32, (tm, tm), 1)
    before = (ui < uj).astype(BF16)
    local = _dg(self32.astype(BF16), before, _NN)
    run = run_ref[...]
    glob = local + run
    run_new = run + jnp.sum(self32, axis=1, keepdims=True)
    run_ref[...] = run_new
    cnt_ref[...] = run_new.astype(jnp.int32)
    e64 = lax.broadcasted_iota(jnp.int32, (N_EXPERTS, tm), 0)
    p_rows = [jnp.sum(jnp.where(e64 == e_rows[k], glob, 0.0), axis=0, keepdims=True) for k in range(TOP_K)]
    pos_ref[...] = jnp.concatenate(p_rows, axis=0).astype(jnp.int32)


def _router(h, w_router_t, bias_col, tm):
    t = h.shape[0]
    return pl.pallas_call(
        functools.partial(_router_kernel, tm=tm),
        grid=(t // tm,),
        in_specs=[pl.BlockSpec((tm, D_MODEL), lambda i: (i, 0)),
                  pl.BlockSpec((N_EXPERTS, D_MODEL), lambda i: (0, 0)),
                  pl.BlockSpec((N_EXPERTS, 1), lambda i: (0, 0))],
        out_specs=[pl.BlockSpec((TOP_K, tm), lambda i: (0, i)),
                   pl.BlockSpec((TOP_K, tm), lambda i: (0, i)),
                   pl.BlockSpec((TOP_K, tm), lambda i: (0, i)),
                   pl.BlockSpec((N_EXPERTS, 1), lambda i: (0, 0))],
        out_shape=[jax.ShapeDtypeStruct((TOP_K, t), jnp.int32),
                   jax.ShapeDtypeStruct((TOP_K, t), jnp.int32),
                   jax.ShapeDtypeStruct((TOP_K, t), F32),
                   jax.ShapeDtypeStruct((N_EXPERTS, 1), jnp.int32)],
        scratch_shapes=[pltpu.VMEM((N_EXPERTS, 1), F32)],
        compiler_params=_params(("arbitrary",)),
        name="router",
    )(h, w_router_t, bias_col)


def _dispatch_kernel(dest_ref, h_ref, xs_in_ref, xs_ref, sem, *, tm):
    del xs_in_ref
    n = tm * TOP_K

    def row_copy(i):
        return pltpu.make_async_copy(h_ref.at[pl.ds(i // TOP_K, 1)],
                                     xs_ref.at[pl.ds(dest_ref[0, 0, i], 1)], sem)

    def start(i, c):
        row_copy(i).start()
        return c

    def wait(i, c):
        row_copy(i).wait()
        return c

    lax.fori_loop(0, n, start, 0)
    lax.fori_loop(0, n, wait, 0)


def _dispatch(h, dest, xs_init, tm):
    t = h.shape[0]
    dest3 = dest.reshape(t // tm, 1, tm * TOP_K)
    return pl.pallas_call(
        functools.partial(_dispatch_kernel, tm=tm),
        grid=(t // tm,),
        in_specs=[pl.BlockSpec((1, 1, tm * TOP_K), lambda i: (i, 0, 0), memory_space=pltpu.SMEM),
                  pl.BlockSpec((tm, D_MODEL), lambda i: (i, 0)),
                  pl.BlockSpec(memory_space=pl.ANY)],
        out_specs=pl.BlockSpec(memory_space=pl.ANY),
        out_shape=jax.ShapeDtypeStruct(xs_init.shape, xs_init.dtype),
        scratch_shapes=[pltpu.SemaphoreType.DMA(())],
        input_output_aliases={2: 0},
        compiler_params=_params(("arbitrary",)),
        name="moe_dispatch",
    )(dest3, h, xs_init)


def _expert_kernel(blk_e_ref, n_used_ref, x_ref, w1_ref, w3_ref, w2_ref, y_ref, w1b, w3b, w2b):
    b = pl.program_id(0)
    used = b < n_used_ref[0]
    prev = blk_e_ref[jnp.maximum(b - 1, 0)]
    fresh = jnp.logical_or(b == 0, blk_e_ref[b] != prev)

    @pl.when(jnp.logical_and(used, fresh))
    def _():
        w1b[...] = w1_ref[...].astype(BF16)
        w3b[...] = w3_ref[...].astype(BF16)
        w2b[...] = w2_ref[...].astype(BF16)

    @pl.when(used)
    def _():
        xb = x_ref[...].astype(BF16)
        h1 = _dg(xb, w1b[...], _NN)
        h3 = _dg(xb, w3b[...], _NN)
        hb = (h1 * _sigmoid(h1) * h3).astype(BF16)
        y_ref[...] = _dg(hb, w2b[...], _NN)

    @pl.when(jnp.logical_not(used))
    def _():
        y_ref[...] = jnp.zeros_like(y_ref)


def _experts(xs, blk_e, n_used, w1, w3, w2):
    cap = xs.shape[0]
    nb = cap // EXPERT_ROWS
    xmap = lambda b, be, nu: (jnp.minimum(b, nu[0] - 1), 0)
    wmap = lambda b, be, nu: (be[b], 0, 0)
    grid_spec = pltpu.PrefetchScalarGridSpec(
        num_scalar_prefetch=2,
        grid=(nb,),
        in_specs=[pl.BlockSpec((EXPERT_ROWS, D_MODEL), xmap),
                  pl.BlockSpec((None, D_MODEL, EXPERT_DIM), wmap),
                  pl.BlockSpec((None, D_MODEL, EXPERT_DIM), wmap),
                  pl.BlockSpec((None, EXPERT_DIM, D_MODEL), wmap)],
        out_specs=pl.BlockSpec((EXPERT_ROWS, D_MODEL), lambda b, be, nu: (b, 0)),
        scratch_shapes=[pltpu.VMEM((D_MODEL, EXPERT_DIM), BF16),
                        pltpu.VMEM((D_MODEL, EXPERT_DIM), BF16),
                        pltpu.VMEM((EXPERT_DIM, D_MODEL), BF16)],
    )
    return pl.pallas_call(
        _expert_kernel,
        grid_spec=grid_spec,
        out_shape=jax.ShapeDtypeStruct((cap, D_MODEL), F32),
        compiler_params=_params(("arbitrary",)),
        name="moe_experts",
    )(blk_e, n_used, xs, w1, w3, w2)


def _combine_kernel(dest_ref, wts_ref, h_ref, ys_ref, ws1_ref, ws3_ref, ws2_ref, g_ref, b_ref,
                    o_ref, gbuf, sem, *, tm):
    n = tm * TOP_K

    def row_copy(i):
        return pltpu.make_async_copy(ys_ref.at[pl.ds(dest_ref[0, 0, i], 1)],
                                     gbuf.at[i % TOP_K, pl.ds(i // TOP_K, 1)], sem)

    def start(i, c):
        row_copy(i).start()
        return c

    def wait(i, c):
        row_copy(i).wait()
        return c

    lax.fori_loop(0, n, start, 0)
    x = h_ref[...]
    xb = x.astype(BF16)
    h1 = _dg(xb, ws1_ref[...], _NN)
    h3 = _dg(xb, ws3_ref[...], _NN)
    acc = _dg((h1 * _sigmoid(h1) * h3).astype(BF16), ws2_ref[...], _NN)
    lax.fori_loop(0, n, wait, 0)
    w = wts_ref[...]
    for k in range(TOP_K):
        acc = acc + gbuf[k] * w[:, k:k + 1]
    o_ref[...] = _layer_norm(ALPHA * x + acc, g_ref[...], b_ref[...])


def _combine(h, ys, dest, wts, ws1, ws3, ws2, g, b, tm):
    t = h.shape[0]
    dest3 = dest.reshape(t // tm, 1, tm * TOP_K)
    row = pl.BlockSpec((1, D_MODEL), lambda i: (0, 0))
    full = lambda a: pl.BlockSpec(a.shape, lambda i: (0, 0))
    return pl.pallas_call(
        functools.partial(_combine_kernel, tm=tm),
        grid=(t // tm,),
        in_specs=[pl.BlockSpec((1, 1, tm * TOP_K), lambda i: (i, 0, 0), memory_space=pltpu.SMEM),
                  pl.BlockSpec((tm, TOP_K), lambda i: (i, 0)),
                  pl.BlockSpec((tm, D_MODEL), lambda i: (i, 0)),
                  pl.BlockSpec(memory_space=pl.ANY),
                  full(ws1), full(ws3), full(ws2), row, row],
        out_specs=pl.BlockSpec((tm, D_MODEL), lambda i: (i, 0)),
        out_shape=jax.ShapeDtypeStruct((t, D_MODEL), F32),
        scratch_shapes=[pltpu.VMEM((TOP_K, tm, D_MODEL), F32), pltpu.SemaphoreType.DMA(())],
        compiler_params=_params(("arbitrary",)),
        name="moe_combine",
    )(dest3, wts, h, ys, ws1, ws3, ws2, g, b)


def _moe(h, router_w, router_bias, w1, w3, w2, ws1, ws3, ws2, ln_g, ln_b):
    t = h.shape[0]
    eidx, pos, wts, counts = _router(h, router_w.T, router_bias.reshape(N_EXPERTS, 1), tm=min(t, 512))
    counts = counts.reshape(N_EXPERTS)
    pcounts = (counts + EXPERT_ROWS - 1) // EXPERT_ROWS * EXPERT_ROWS
    pends = jnp.cumsum(pcounts)
    pstarts = pends - pcounts
    dest = (pstarts[eidx] + pos).T.reshape(-1)
    nb = t * TOP_K // EXPERT_ROWS + N_EXPERTS
    blk_start = jnp.arange(nb, dtype=jnp.int32) * EXPERT_ROWS
    blk_e = jnp.minimum(jnp.sum(blk_start[:, None] >= pends[None, :], axis=1), N_EXPERTS - 1).astype(jnp.int32)
    n_used = (pends[-1] // EXPERT_ROWS).astype(jnp.int32).reshape(1)
    xs = _dispatch(h, dest, jnp.zeros((nb * EXPERT_ROWS, D_MODEL), F32), tm=min(t, 256))
    ys = _experts(xs, blk_e, n_used, w1, w3, w2)
    return _combine(h, ys, dest, wts.T, ws1.astype(BF16), ws3.astype(BF16), ws2.astype(BF16),
                    ln_g.reshape(1, -1), ln_b.reshape(1, -1), tm=min(t, 128))


def _diffattn_kernel(q_ref, k_ref, v_ref, lam_ref, sg_ref, o_ref, qs_ref, m_ref, l_ref, acc_ref,
                     *, tq, tk, lam_init):
    qi = pl.program_id(1)
    ki = pl.program_id(2)
    nk = pl.num_programs(2)

    @pl.when(ki == 0)
    def _():
        q = q_ref[...]
        map0 = lax.broadcasted_iota(jnp.int32, q.shape, 1) < DIFF_HEAD_DIM
        zero = jnp.zeros_like(q)
        qs_ref[:tq, :] = jnp.where(map0, q, zero)
        qs_ref[tq:, :] = jnp.where(map0, zero, q)
        m_ref[...] = jnp.full_like(m_ref, -jnp.inf)
        l_ref[...] = jnp.zeros_like(l_ref)
        acc_ref[...] = jnp.zeros_like(acc_ref)

    @pl.when(ki * tk <= qi * tq + tq - 1)
    def _():
        s = _dg(qs_ref[...], k_ref[...], _NT) * (DIFF_HEAD_DIM ** -0.5)
        row = lax.broadcasted_iota(jnp.int32, s.shape, 0)
        qpos = qi * tq + jnp.where(row >= tq, row - tq, row)
        kpos = ki * tk + lax.broadcasted_iota(jnp.int32, s.shape, 1)
        s = jnp.where(kpos <= qpos, s, -jnp.inf)
        m_old = m_ref[...]
        m_new = jnp.maximum(m_old, jnp.max(s, axis=-1, keepdims=True))
        p = jnp.exp(s - m_new)
        alpha = jnp.exp(m_old - m_new)
        l_ref[...] = alpha * l_ref[...] + jnp.sum(p, axis=-1, keepdims=True)
        acc_ref[...] = alpha * acc_ref[...] + _dg(p.astype(BF16), v_ref[...], _NN)
        m_ref[...] = m_new

    @pl.when(ki == nk - 1)
    def _():
        lp = lam_ref[...]
        lam = (jnp.exp(jnp.sum(lp[0:1] * lp[1:2], axis=-1, keepdims=True))
               - jnp.exp(jnp.sum(lp[2:3] * lp[3:4], axis=-1, keepdims=True)) + lam_init)
        o = acc_ref[:tq, :] / l_ref[:tq, :] - lam * (acc_ref[tq:, :] / l_ref[tq:, :])
        o = o * lax.rsqrt(jnp.mean(o * o, axis=-1, keepdims=True) + RMS_EPS)
        o_ref[...] = o * sg_ref[...] * (1.0 - lam_init)


def _diffattn(q, kv, lam_params, subln_g, lam_init, tq, tk):
    t = q.shape[0]
    nh = DIFF_HEADS
    last = lambda qi: (qi * tq + tq - 1) // tk
    return pl.pallas_call(
        functools.partial(_diffattn_kernel, tq=tq, tk=tk, lam_init=lam_init),
        grid=(nh, t // tq, t // tk),
        in_specs=[pl.BlockSpec((tq, LANES), lambda h, qi, ki: (qi, h)),
                  pl.BlockSpec((tk, LANES), lambda h, qi, ki: (jnp.minimum(ki, last(qi)), h)),
                  pl.BlockSpec((tk, LANES), lambda h, qi, ki: (jnp.minimum(ki, last(qi)), nh + h)),
                  pl.BlockSpec(lam_params.shape, lambda h, qi, ki: (0, 0)),
                  pl.BlockSpec((1, LANES), lambda h, qi, ki: (0, 0))],
        out_specs=pl.BlockSpec((tq, LANES), lambda h, qi, ki: (qi, h)),
        out_shape=jax.ShapeDtypeStruct((t, MAIN_WIDTH), F32),
        scratch_shapes=[pltpu.VMEM((2 * tq, LANES), BF16),
                        pltpu.VMEM((2 * tq, 1), F32),
                        pltpu.VMEM((2 * tq, 1), F32),
                        pltpu.VMEM((2 * tq, LANES), F32)],
        compiler_params=_params(("parallel", "parallel", "arbitrary")),
        name="diff_attn",
    )(q, kv, kv, lam_params, subln_g)


def _pad_rows(w, rows):
    return jnp.pad(w, ((0, rows - w.shape[0]), (0, 0)))


def _layer_a(h, memkv, a_w_in, a_mu, a_w0, a_w_decay_up, a_a0, a_w_a_up, a_w_g_up, a_k_k, a_k_a,
             a_r_k, a_gn_g, a_gn_b, w_out, ln1_g, ln1_b, tm):
    c = MAIN_WIDTH
    lo = 3 * c

    def pad_cols(m):
        pieces = [m[..., :lo],
                  jnp.pad(m[..., lo:lo + DECAY_LORA], ((0, 0), (0, LANES - DECAY_LORA))),
                  jnp.pad(m[..., lo + DECAY_LORA:lo + DECAY_LORA + AAA_LORA], ((0, 0), (0, LANES - AAA_LORA))),
                  jnp.pad(m[..., lo + DECAY_LORA + AAA_LORA:lo + DECAY_LORA + AAA_LORA + GATE_LORA],
                          ((0, 0), (0, 2 * LANES - GATE_LORA))),
                  m[..., lo + DECAY_LORA + AAA_LORA + GATE_LORA:]]
        return jnp.concatenate(pieces, axis=-1)

    w_in = pad_cols(a_w_in).astype(BF16)
    mu_pad = pad_cols(jnp.concatenate([a_mu, jnp.zeros((MEM_WIDTH,), F32)]).reshape(1, -1))[:, :lo + LORA_TILE]
    z = _proj(h, w_in, tm=tm, tn=512, out_dtype=F32)
    lane = jnp.arange(LANES)
    g128 = (lane[:, None] // RWKV_HEAD_DIM == lane[None, :] // RWKV_HEAD_DIM).astype(BF16)
    row = lambda a: a.reshape(1, -1)
    r, lw, k, v, an, b, g = _rwkv_prep(
        z, mu_pad, row(a_w0), _pad_rows(a_w_decay_up, LANES), row(a_a0), _pad_rows(a_w_a_up, LANES),
        _pad_rows(a_w_g_up, 2 * LANES), row(a_k_k), row(a_k_a), g128, tm=min(tm, 256))
    main = _wkv(r, lw, k, v, an, b, g, row(a_r_k), row(a_gn_g), row(a_gn_b), g128)
    memo = _memattn(z, (lo + LORA_TILE) // MEM_WIDTH, memkv, tm=min(tm, 512))
    return _outproj_ln(main, memo, h, w_out.astype(BF16), row(ln1_g), row(ln1_b), tm=min(tm, 256))


def _layer_b(h, memkv, positions, kv_shared_w, b_w_in, lam_params, subln_g, lam_init, w_out, ln1_g, ln1_b, tm):
    c = MAIN_WIDTH
    rope = _rope_tables(positions, tm=min(tm, 512))
    kv = _proj(h, kv_shared_w.astype(BF16), tm=tm, tn=512, out_dtype=BF16, rope=rope, n_rope=c // 512)
    zq = _proj(h, b_w_in.astype(BF16), tm=tm, tn=512, out_dtype=BF16, rope=rope, n_rope=c // 512)
    tq = min(tm, 256)
    main = _diffattn(zq, kv, lam_params, jnp.tile(subln_g, 1).reshape(1, -1), lam_init, tq=tq, tk=min(tm, 512))
    memo = _memattn(zq, c // MEM_WIDTH, memkv, tm=min(tm, 512))
    row = lambda a: a.reshape(1, -1)
    return _outproj_ln(main, memo, h, w_out.astype(BF16), row(ln1_g), row(ln1_b), tm=min(tm, 256))


def kernel(x, mem, positions, a_w_in, a_mu, a_w0, a_w_decay_up, a_a0, a_w_a_up, a_w_g_up, a_k_k, a_k_a, a_r_k, a_gn_g, a_gn_b, kv_shared_w, b_w_in, b_lambda_q1, b_lambda_k1, b_lambda_q2, b_lambda_k2, b_subln_g, mem_w_kv, w_out, ln1_g, ln1_b, ln2_g, ln2_b, router_w, router_bias, exp_w1, exp_w3, exp_w2, sh_w1, sh_w3, sh_w2):
    bsz, t, _ = x.shape
    assert bsz == 1
    h = x.reshape(t, D_MODEL)
    tm = min(t, 1024)
    mem2 = mem.reshape(-1, D_MODEL)
    n_a = DEPTH // 2
    for l in range(DEPTH):
        memkv = _proj(mem2, mem_w_kv[l].astype(BF16), tm=mem2.shape[0], tn=512, out_dtype=F32)
        if l < n_a:
            h = _layer_a(h, memkv, a_w_in[l], a_mu[l], a_w0[l], a_w_decay_up[l], a_a0[l], a_w_a_up[l],
                         a_w_g_up[l], a_k_k[l], a_k_a[l], a_r_k[l], a_gn_g[l], a_gn_b[l],
                         w_out[l], ln1_g[l], ln1_b[l], tm)
        else:
            j = l - n_a
            lam_init = 0.8 - 0.6 * math.exp(-0.3 * l)
            lam_params = jnp.stack([b_lambda_q1[j], b_lambda_k1[j], b_lambda_q2[j], b_lambda_k2[j]])
            h = _layer_b(h, memkv, positions, kv_shared_w, b_w_in[j], lam_params, b_subln_g[j], lam_init,
                         w_out[l], ln1_g[l], ln1_b[l], tm)
        h = _moe(h, router_w[l], router_bias[l], exp_w1[l], exp_w3[l], exp_w2[l],
                 sh_w1[l], sh_w3[l], sh_w2[l], ln2_g[l], ln2_b[l])
    return h.reshape(bsz, t, D_MODEL)
```

```python
import functools
import math

import jax
import jax.numpy as jnp
from jax import lax
from jax.experimental import pallas as pl
from jax.experimental.pallas import tpu as pltpu

F32 = jnp.float32
BF16 = jnp.bfloat16

D_MODEL = 2048
DEPTH = 2
MEM_HEADS = 4
MEM_WIDTH = D_MODEL // 4
MEM_HEAD_DIM = MEM_WIDTH // MEM_HEADS
MAIN_WIDTH = D_MODEL - MEM_WIDTH
RWKV_HEAD_DIM = 64
DECAY_LORA = 64
AAA_LORA = 64
GATE_LORA = 224
GN_EPS = 64e-5
DIFF_HEADS = 12
DIFF_HEAD_DIM = 64
ROPE_DIM = DIFF_HEAD_DIM // 4
ROPE_THETA = 500000.0
N_EXPERTS = 64
TOP_K = 8
N_GROUPS = 8
GROUP_SIZE = N_EXPERTS // N_GROUPS
TOPK_GROUPS = 4
EXPERT_DIM = D_MODEL // 4
ROUTED_SCALE = 2.5
ALPHA = (2 * DEPTH) ** 0.25
LN_EPS = 1e-5
RMS_EPS = 1e-5

LANES = 128
VMEM_LIMIT = 56 * 1024 * 1024

LORA_TILE = 512
A_IN_PAD = 3 * MAIN_WIDTH + LORA_TILE + MEM_WIDTH
WKV_CHUNK = 64
WKV_PAIRS = 4
EXPERT_ROWS = 256
ATTN_TILE = 512


def _params(sem):
    return pltpu.CompilerParams(dimension_semantics=sem, vmem_limit_bytes=VMEM_LIMIT)


def _split2(x):
    hi = x.astype(BF16)
    lo = (x - hi.astype(F32)).astype(BF16)
    return hi, lo


def _dg(a, b, dims):
    return lax.dot_general(a, b, (dims, ((), ())), preferred_element_type=F32)


_NN = ((1,), (0,))
_NT = ((1,), (1,))
_TN = ((0,), (0,))


def _dot3(a, b, dims=_NN):
    ah, al = _split2(a)
    bh, bl = _split2(b)
    return _dg(ah, bh, dims) + (_dg(ah, bl, dims) + _dg(al, bh, dims))


def _dot2l(a, b_exact, dims=_NN):
    ah, al = _split2(a)
    return _dg(ah, b_exact, dims) + _dg(al, b_exact, dims)


def _dot1(a, b, dims=_NN):
    return _dg(a.astype(BF16), b.astype(BF16), dims)


def _sigmoid(x):
    return 1.0 / (1.0 + jnp.exp(-x))


def _proj_kernel(*refs, n_rope, tn, cast_x):
    if n_rope:
        x_ref, w_ref, c_ref, sa_ref, sb_ref, o_ref = refs[:6]
        rest = refs[6:]
    else:
        x_ref, w_ref, o_ref = refs[:3]
        rest = refs[3:]
    j = pl.program_id(1)
    if cast_x:
        xb_ref, = rest

        @pl.when(j == 0)
        def _():
            xb_ref[...] = x_ref[...].astype(BF16)
        xb = xb_ref[...]
    else:
        xb = x_ref[...]
    acc = jnp.dot(xb, w_ref[...], preferred_element_type=F32)
    if n_rope:
        @pl.when(j < n_rope)
        def _():
            c, sa, sb = c_ref[...], sa_ref[...], sb_ref[...]
            for q in range(tn // LANES):
                a = acc[:, q * LANES:(q + 1) * LANES]
                rot = a * c + pltpu.roll(a, LANES - ROPE_DIM // 2, 1) * sa + pltpu.roll(a, ROPE_DIM // 2, 1) * sb
                o_ref[:, q * LANES:(q + 1) * LANES] = rot.astype(o_ref.dtype)

        @pl.when(j >= n_rope)
        def _():
            o_ref[...] = acc.astype(o_ref.dtype)
    else:
        o_ref[...] = acc.astype(o_ref.dtype)


def _proj(x, w, *, tm, tn, out_dtype, rope=None, n_rope=0):
    m, k = x.shape
    n = w.shape[1]
    assert m % tm == 0 and n % tn == 0
    cast_x = x.dtype != BF16
    in_specs = [pl.BlockSpec((tm, k), lambda i, j: (i, 0)),
                pl.BlockSpec((k, tn), lambda i, j: (0, j))]
    args = [x, w]
    if n_rope:
        in_specs += [pl.BlockSpec((tm, LANES), lambda i, j: (i, 0))] * 3
        args += list(rope)
    return pl.pallas_call(
        functools.partial(_proj_kernel, n_rope=n_rope, tn=tn, cast_x=cast_x),
        grid=(m // tm, n // tn),
        in_specs=in_specs,
        out_specs=pl.BlockSpec((tm, tn), lambda i, j: (i, j)),
        out_shape=jax.ShapeDtypeStruct((m, n), out_dtype),
        scratch_shapes=[pltpu.VMEM((tm, k), BF16)] if cast_x else [],
        compiler_params=_params(("parallel", "arbitrary")),
        name="proj",
    )(*args)


def _rope_table_kernel(pos_ref, invf_ref, c_ref, sa_ref, sb_ref, *, scale):
    ang = pos_ref[...].astype(F32) * invf_ref[...]
    lane = lax.broadcasted_iota(jnp.int32, ang.shape, 1) & (DIFF_HEAD_DIM - 1)
    c = jnp.cos(ang)
    s = jnp.sin(ang)
    half = ROPE_DIM // 2
    c_ref[...] = jnp.where(lane < ROPE_DIM, c, 1.0) * scale
    sa_ref[...] = jnp.where(lane < half, -s, 0.0) * scale
    sb_ref[...] = jnp.where(lane < half, 0.0, jnp.where(lane < ROPE_DIM, s, 0.0)) * scale


def _rope_tables(positions, tm, scale):
    t = positions.shape[1]
    half = ROPE_DIM // 2
    inv_freq = ROPE_THETA ** (-jnp.arange(half, dtype=F32) / half)
    lane = jnp.arange(LANES) % DIFF_HEAD_DIM
    invf = jnp.where(lane < ROPE_DIM, inv_freq[lane % half], 0.0).reshape(1, LANES).astype(F32)
    pos = positions.reshape(t, 1)
    shp = jax.ShapeDtypeStruct((t, LANES), F32)
    return pl.pallas_call(
        functools.partial(_rope_table_kernel, scale=scale),
        grid=(t // tm,),
        in_specs=[pl.BlockSpec((tm, 1), lambda i: (i, 0)), pl.BlockSpec((1, LANES), lambda i: (0, 0))],
        out_specs=[pl.BlockSpec((tm, LANES), lambda i: (i, 0))] * 3,
        out_shape=[shp, shp, shp],
        compiler_params=_params(("parallel",)),
        name="rope_tables",
    )(pos, invf)


def _head_sum(x, g128):
    return _dot2l(x, g128)


def _prep_kernel(z_ref, mu_ref, w0_ref, wdu_ref, a0_ref, wau_ref, wgu_ref, kk_ref, ka_ref, g128_ref,
                 r_o, lw_o, k_o, v_o, an_o, b_o, g_o, carry_ref, *, tm):
    i = pl.program_id(0)

    @pl.when(i == 0)
    def _():
        carry_ref[...] = jnp.zeros_like(carry_ref)

    row0 = lax.broadcasted_iota(jnp.int32, (tm, 1), 0) == 0
    c = MAIN_WIDTH

    def shifted(c0, c1):
        z = z_ref[:, c0:c1]
        prev = jnp.where(row0, carry_ref[:, c0:c1], pltpu.roll(z, 1, 0))
        return z + (prev - z) * mu_ref[:, c0:c1]

    r = shifted(0, c)
    k = shifted(c, 2 * c)
    v = shifted(2 * c, 3 * c)
    lora = shifted(3 * c, 3 * c + LORA_TILE)
    carry_ref[...] = z_ref[pl.ds(tm - 1, 1), :]

    dw = lora[:, 0:LANES]
    da = lora[:, LANES:2 * LANES]
    dg = lora[:, 2 * LANES:4 * LANES]
    wl = w0_ref[...] + _dot3(jnp.tanh(dw), wdu_ref[...])
    lw = -_sigmoid(wl) * math.exp(-0.5)
    a = _sigmoid(a0_ref[...] + _dot3(da, wau_ref[...]))
    g = _dot3(_sigmoid(dg), wgu_ref[...])
    kk = k * kk_ref[...]
    g128 = g128_ref[...]
    for q in range(c // LANES):
        sl = slice(q * LANES, (q + 1) * LANES)
        kq = kk[:, sl]
        nrm = jnp.maximum(jnp.sqrt(_head_sum(kq * kq, g128)), 1e-12)
        kn = kq / nrm
        an_o[:, sl] = -kn
        b_o[:, sl] = kn * a[:, sl]
    r_o[...] = r
    lw_o[...] = lw
    k_o[...] = k * (1.0 + (a - 1.0) * ka_ref[...])
    v_o[...] = v
    g_o[...] = g


def _rwkv_prep(z, mu_pad, w0, wdu_pad, a0, wau_pad, wgu_pad, k_k, k_a, g128, tm):
    t = z.shape[0]
    c = MAIN_WIDTH
    wz = 3 * c + LORA_TILE
    row = lambda n: pl.BlockSpec((1, n), lambda i: (0, 0))
    full = lambda a: pl.BlockSpec(a.shape, lambda i: (0, 0))
    shp = jax.ShapeDtypeStruct((t, c), F32)
    return pl.pallas_call(
        functools.partial(_prep_kernel, tm=tm),
        grid=(t // tm,),
        in_specs=[pl.BlockSpec((tm, wz), lambda i: (i, 0)), row(wz), row(c), full(wdu_pad), row(c),
                  full(wau_pad), full(wgu_pad), row(c), row(c), full(g128)],
        out_specs=[pl.BlockSpec((tm, c), lambda i: (i, 0))] * 7,
        out_shape=[shp] * 7,
        scratch_shapes=[pltpu.VMEM((1, wz), F32)],
        compiler_params=_params(("arbitrary",)),
        name="rwkv_prep",
    )(z, mu_pad, w0, wdu_pad, a0, wau_pad, wgu_pad, k_k, k_a, g128)


def _wkv_kernel(r_ref, lw_ref, k_ref, v_ref, an_ref, b_ref, g_ref, rk_ref, gng_ref, gnb_ref, g128_ref,
                o_ref, ht_ref, *, L, pairs):
    ci = pl.program_id(1)

    @pl.when(ci == 0)
    def _():
        ht_ref[...] = jnp.zeros_like(ht_ref)

    P = range(pairs)
    cols = [slice(p * LANES, (p + 1) * LANES) for p in P]
    ld = lambda ref: [ref[:, c] for c in cols]
    r, lw, k, v, an, b = ld(r_ref), ld(lw_ref), ld(k_ref), ld(v_ref), ld(an_ref), ld(b_ref)
    g128 = g128_ref[...]
    n2 = 2 * L

    rowi = lax.broadcasted_iota(jnp.int32, (L, L), 0)
    coli = lax.broadcasted_iota(jnp.int32, (L, L), 1)
    tri = (coli <= rowi).astype(BF16)

    def running_sum(x):
        x_h = x.astype(BF16)
        x_r = x - x_h.astype(F32)
        x_m = x_r.astype(BF16)
        x_l = (x_r - x_m.astype(F32)).astype(BF16)
        return _dg(tri, x_h, _NN) + (_dg(tri, x_m, _NN) + _dg(tri, x_l, _NN))

    cum = [running_sum(lw[p]) for p in P]
    cum_last = [cum[p][L - 1:L, :] for p in P]
    p_inv = [jnp.exp(-cum[p]) for p in P]
    at = [an[p] * jnp.exp(cum[p] - lw[p]) for p in P]
    rt = [r[p] * jnp.exp(cum[p]) for p in P]
    bt = [b[p] * p_inv[p] for p in P]
    kt = [k[p] * p_inv[p] for p in P]
    p_end = [jnp.exp(cum_last[p] - cum[p]) for p in P]

    head0 = lax.broadcasted_iota(jnp.int32, (L, LANES), 1) < RWKV_HEAD_DIM

    def stack(x):
        return jnp.concatenate([jnp.where(head0, x, 0.0), jnp.where(head0, 0.0, x)], axis=0)

    ar = [jnp.concatenate([stack(at[p]), stack(rt[p])], axis=0) for p in P]
    bk = [jnp.concatenate([stack(bt[p]), stack(kt[p])], axis=0) for p in P]
    v_st = [stack(v[p]) for p in P]
    m = [_dot3(ar[p], bk[p], _NT) for p in P]

    ri = lax.broadcasted_iota(jnp.int32, (n2, n2), 0)
    cj = lax.broadcasted_iota(jnp.int32, (n2, n2), 1)
    same = (ri >= L) == (cj >= L)
    strict = same & (cj < ri)
    incl = same & (cj <= ri)
    ab = [jnp.where(strict, m[p][:n2, :n2], 0.0) for p in P]
    akrk = [jnp.concatenate([jnp.where(strict, m[p][:n2, n2:], 0.0),
                             jnp.where(incl, m[p][n2:, n2:], 0.0)], axis=0) for p in P]
    rb = [jnp.where(incl, m[p][n2:, :n2], 0.0) for p in P]

    ht = [ht_ref[p] for p in P]
    arh = [_dot3(ar[p], ht[p], _NT) for p in P]
    kv = [_dot3(akrk[p], v_st[p]) for p in P]
    u = [arh[p][:n2] + kv[p][:n2] for p in P]
    apow = ab
    steps = max(1, (L - 1).bit_length())
    for s in range(steps):
        u = [u[p] + _dot3(apow[p], u[p]) for p in P]
        if s + 1 < steps:
            apow = [_dot3(apow[p], apow[p]) for p in P]
    y_st = [arh[p][n2:] + kv[p][n2:] + _dot3(rb[p], u[p]) for p in P]
    y = [y_st[p][:L] + y_st[p][L:] for p in P]

    bk_end = [jnp.concatenate([stack(b[p] * p_end[p]), stack(k[p] * p_end[p])], axis=0) for p in P]
    for p in P:
        uv = jnp.concatenate([u[p], v_st[p]], axis=0)
        ht_ref[p] = ht[p] * jnp.exp(cum_last[p]) + _dot3(uv, bk_end[p], _TN)

    inv_n = 1.0 / RWKV_HEAD_DIM
    mu = [_head_sum(y[p], g128) * inv_n for p in P]
    yc = [y[p] - mu[p] for p in P]
    var = [_head_sum(yc[p] * yc[p], g128) * inv_n for p in P]
    bonus = [_head_sum(r[p] * k[p] * rk_ref[:, cols[p]], g128) * v[p] for p in P]
    for p in P:
        yn = yc[p] * lax.rsqrt(var[p] + GN_EPS) * gng_ref[:, cols[p]] + gnb_ref[:, cols[p]]
        o_ref[:, cols[p]] = (yn + bonus[p]) * g_ref[:, cols[p]]


def _wkv(r, lw, k, v, an, b, g, r_k, gn_g, gn_b, g128):
    t, c = r.shape
    L = WKV_CHUNK
    w = WKV_PAIRS * LANES
    tok = pl.BlockSpec((L, w), lambda p, ci: (ci, p))
    par = pl.BlockSpec((1, w), lambda p, ci: (0, p))
    return pl.pallas_call(
        functools.partial(_wkv_kernel, L=L, pairs=WKV_PAIRS),
        grid=(c // w, t // L),
        in_specs=[tok] * 7 + [par] * 3 + [pl.BlockSpec((LANES, LANES), lambda p, ci: (0, 0))],
        out_specs=tok,
        out_shape=jax.ShapeDtypeStruct((t, c), F32),
        scratch_shapes=[pltpu.VMEM((WKV_PAIRS, LANES, LANES), F32)],
        compiler_params=_params(("parallel", "arbitrary")),
        name="wkv7",
    )(r, lw, k, v, an, b, g, r_k, gn_g, gn_b, g128)


def _memattn_kernel(q_ref, kv_ref, o_ref):
    scale = MEM_HEAD_DIM ** -0.5
    for hh in range(MEM_HEADS):
        sl = slice(hh * MEM_HEAD_DIM, (hh + 1) * MEM_HEAD_DIM)
        q = q_ref[:, sl].astype(BF16)
        kh = kv_ref[:, sl].astype(BF16)
        vh = kv_ref[:, MEM_WIDTH + hh * MEM_HEAD_DIM:MEM_WIDTH + (hh + 1) * MEM_HEAD_DIM].astype(BF16)
        s = _dg(q, kh, _NT) * scale
        s = s - jnp.max(s, axis=-1, keepdims=True)
        e = jnp.exp(s)
        p = e / jnp.sum(e, axis=-1, keepdims=True)
        o_ref[:, sl] = _dg(p.astype(BF16), vh, _NN)


def _memattn(zq, q_col_block, memkv, tm):
    t = zq.shape[0]
    return pl.pallas_call(
        _memattn_kernel,
        grid=(t // tm,),
        in_specs=[pl.BlockSpec((tm, MEM_WIDTH), lambda i: (i, q_col_block)),
                  pl.BlockSpec(memkv.shape, lambda i: (0, 0))],
        out_specs=pl.BlockSpec((tm, MEM_WIDTH), lambda i: (i, 0)),
        out_shape=jax.ShapeDtypeStruct((t, MEM_WIDTH), F32),
        compiler_params=_params(("parallel",)),
        name="mem_attn",
    )(zq, memkv)


def _layer_norm(x, g, b):
    mu = jnp.mean(x, axis=-1, keepdims=True)
    xc = x - mu
    var = jnp.mean(xc * xc, axis=-1, keepdims=True)
    return xc * lax.rsqrt(var + LN_EPS) * g + b


def _outproj_kernel(main_ref, memo_ref, h_ref, w_ref, g_ref, b_ref, o_ref):
    mix = (_dg(main_ref[...].astype(BF16), w_ref[:MAIN_WIDTH, :], _NN)
           + _dg(memo_ref[...].astype(BF16), w_ref[MAIN_WIDTH:, :], _NN))
    o_ref[...] = _layer_norm(ALPHA * h_ref[...] + mix, g_ref[...], b_ref[...])


def _outproj_ln(main, memo, h, w_bf, g, b, tm):
    t = h.shape[0]
    row = pl.BlockSpec((1, D_MODEL), lambda i: (0, 0))
    return pl.pallas_call(
        _outproj_kernel,
        grid=(t // tm,),
        in_specs=[pl.BlockSpec((tm, MAIN_WIDTH), lambda i: (i, 0)),
                  pl.BlockSpec((tm, MEM_WIDTH), lambda i: (i, 0)),
                  pl.BlockSpec((tm, D_MODEL), lambda i: (i, 0)),
                  pl.BlockSpec((D_MODEL, D_MODEL), lambda i: (0, 0)), row, row],
        out_specs=pl.BlockSpec((tm, D_MODEL), lambda i: (i, 0)),
        out_shape=jax.ShapeDtypeStruct((t, D_MODEL), F32),
        compiler_params=_params(("parallel",)),
        name="outproj_ln",
    )(main, memo, h, w_bf, g, b)


def _first_max(vals, ids):
    m = vals[0]
    for x in vals[1:]:
        m = jnp.maximum(m, x)
    m = jnp.max(m, axis=0, keepdims=True)
    big = jnp.int32(1 << 20)
    idx = None
    for x, e in zip(vals, ids):
        cand = jnp.where(x == m, e, big)
        idx = cand if idx is None else jnp.minimum(idx, cand)
    idx = jnp.min(idx, axis=0, keepdims=True)
    return m, idx


def _router_kernel(h_ref, wt_ref, bias_ref, eidx_ref, pos_ref, wts_ref, cnt_ref, run_ref, *, tm):
    i = pl.program_id(0)

    @pl.when(i == 0)
    def _():
        run_ref[...] = jnp.zeros_like(run_ref)

    neg = -jnp.inf
    logits = _dot3(wt_ref[...], h_ref[...], _NT)
    scores = _sigmoid(logits)
    biased = scores + bias_ref[...]
    sub = lax.broadcasted_iota(jnp.int32, (GROUP_SIZE, tm), 0)
    grp = [biased[g * GROUP_SIZE:(g + 1) * GROUP_SIZE, :] for g in range(N_GROUPS)]
    sc = [scores[g * GROUP_SIZE:(g + 1) * GROUP_SIZE, :] for g in range(N_GROUPS)]
    ids = [sub + g * GROUP_SIZE for g in range(N_GROUPS)]

    gs = []
    for g in range(N_GROUPS):
        m1, i1 = _first_max([grp[g]], [sub])
        m2 = jnp.max(jnp.where(sub == i1, neg, grp[g]), axis=0, keepdims=True)
        gs.append(m1 + m2)
    gscore = jnp.concatenate(gs, axis=0)
    gmask = jnp.zeros((N_GROUPS, tm), jnp.bool_)
    cur = gscore
    for _ in range(TOPK_GROUPS):
        _, gi = _first_max([cur], [sub])
        hit = sub == gi
        gmask = gmask | hit
        cur = jnp.where(hit, neg, cur)
    gm = gmask.astype(F32)
    cur = [jnp.where(gm[g:g + 1, :] > 0.0, grp[g], neg) for g in range(N_GROUPS)]

    sel = [jnp.zeros((GROUP_SIZE, tm), jnp.bool_) for _ in range(N_GROUPS)]
    e_rows, w_rows = [], []
    for _ in range(TOP_K):
        _, ei = _first_max(cur, ids)
        w = jnp.zeros((1, tm), F32)
        for g in range(N_GROUPS):
            hit = ids[g] == ei
            sel[g] = sel[g] | hit
            cur[g] = jnp.where(hit, neg, cur[g])
            w = w + jnp.sum(jnp.where(hit, sc[g], 0.0), axis=0, keepdims=True)
        e_rows.append(ei)
        w_rows.append(w)
    eidx = jnp.concatenate(e_rows, axis=0)
    wraw = jnp.concatenate(w_rows, axis=0)
    wts_ref[...] = wraw / (jnp.sum(wraw, axis=0, keepdims=True) + 1e-20) * ROUTED_SCALE
    eidx_ref[...] = eidx

    self32 = jnp.concatenate([s.astype(F32) for s in sel], axis=0)
    ui = lax.broadcasted_iota(jnp.int32, (tm, tm), 0)
    uj = lax.broadcasted_iota(jnp.int32, (tm, tm), 1)
    before = (ui < uj).astype(BF16)
    local = _dg(self32.astype(BF16), before, _NN)
    run = run_ref[...]
    glob = local + run
    run_new = run + jnp.sum(self32, axis=1, keepdims=True)
    run_ref[...] = run_new
    cnt_ref[...] = run_new.astype(jnp.int32)
    e64 = lax.broadcasted_iota(jnp.int32, (N_EXPERTS, tm), 0)
    p_rows = [jnp.sum(jnp.where(e64 == e_rows[k], glob, 0.0), axis=0, keepdims=True) for k in range(TOP_K)]
    pos_ref[...] = jnp.concatenate(p_rows, axis=0).astype(jnp.int32)


def _router(h, w_router_t, bias_col, tm):
    t = h.shape[0]
    return pl.pallas_call(
        functools.partial(_router_kernel, tm=tm),
        grid=(t // tm,),
        in_specs=[pl.BlockSpec((tm, D_MODEL), lambda i: (i, 0)),
                  pl.BlockSpec((N_EXPERTS, D_MODEL), lambda i: (0, 0)),
                  pl.BlockSpec((N_EXPERTS, 1), lambda i: (0, 0))],
        out_specs=[pl.BlockSpec((TOP_K, tm), lambda i: (0, i)),
                   pl.BlockSpec((TOP_K, tm), lambda i: (0, i)),
                   pl.BlockSpec((TOP_K, tm), lambda i: (0, i)),
                   pl.BlockSpec((N_EXPERTS, 1), lambda i: (0, 0))],
        out_shape=[jax.ShapeDtypeStruct((TOP_K, t), jnp.int32),
                   jax.ShapeDtypeStruct((TOP_K, t), jnp.int32),
                   jax.ShapeDtypeStruct((TOP_K, t), F32),
                   jax.ShapeDtypeStruct((N_EXPERTS, 1), jnp.int32)],
        scratch_shapes=[pltpu.VMEM((N_EXPERTS, 1), F32)],
        compiler_params=_params(("arbitrary",)),
        name="router",
    )(h, w_router_t, bias_col)


def _dispatch_kernel(dest_ref, h_ref, xs_in_ref, xs_ref, sem, *, tm):
    del xs_in_ref

    def row_copy(t, k):
        return pltpu.make_async_copy(h_ref.at[pl.ds(t, 1)],
                                     xs_ref.at[pl.ds(dest_ref[0, 0, t * TOP_K + k], 1)], sem)

    def start(t, c):
        for k in range(TOP_K):
            row_copy(t, k).start()
        return c

    def wait(t, c):
        for k in range(TOP_K):
            row_copy(t, k).wait()
        return c

    lax.fori_loop(0, tm, start, 0)
    lax.fori_loop(0, tm, wait, 0)


def _dispatch(h, dest, xs_init, tm):
    t = h.shape[0]
    dest3 = dest.reshape(t // tm, 1, tm * TOP_K)
    return pl.pallas_call(
        functools.partial(_dispatch_kernel, tm=tm),
        grid=(t // tm,),
        in_specs=[pl.BlockSpec((1, 1, tm * TOP_K), lambda i: (i, 0, 0), memory_space=pltpu.SMEM),
                  pl.BlockSpec((tm, D_MODEL), lambda i: (i, 0)),
                  pl.BlockSpec(memory_space=pl.ANY)],
        out_specs=pl.BlockSpec(memory_space=pl.ANY),
        out_shape=jax.ShapeDtypeStruct(xs_init.shape, xs_init.dtype),
        scratch_shapes=[pltpu.SemaphoreType.DMA(())],
        input_output_aliases={2: 0},
        compiler_params=_params(("arbitrary",)),
        name="moe_dispatch",
    )(dest3, h, xs_init)


def _expert_kernel(blk_e_ref, n_used_ref, x_ref, w1_ref, w3_ref, w2_ref, y_ref, w1b, w3b, w2b):
    b = pl.program_id(0)
    used = b < n_used_ref[0]
    prev = blk_e_ref[jnp.maximum(b - 1, 0)]
    fresh = jnp.logical_or(b == 0, blk_e_ref[b] != prev)

    @pl.when(jnp.logical_and(used, fresh))
    def _():
        w1b[...] = w1_ref[...].astype(BF16)
        w3b[...] = w3_ref[...].astype(BF16)
        w2b[...] = w2_ref[...].astype(BF16)

    @pl.when(used)
    def _():
        xb = x_ref[...].astype(BF16)
        h1 = _dg(xb, w1b[...], _NN)
        h3 = _dg(xb, w3b[...], _NN)
        hb = (h1 * _sigmoid(h1) * h3).astype(BF16)
        y_ref[...] = _dg(hb, w2b[...], _NN)

    @pl.when(jnp.logical_not(used))
    def _():
        y_ref[...] = jnp.zeros_like(y_ref)


def _experts(xs, blk_e, n_used, w1, w3, w2, layer):
    cap = xs.shape[0]
    nb = cap // EXPERT_ROWS
    xmap = lambda b, be, nu: (jnp.minimum(b, nu[0] - 1), 0)
    wmap = lambda b, be, nu: (layer, be[b], 0, 0)
    grid_spec = pltpu.PrefetchScalarGridSpec(
        num_scalar_prefetch=2,
        grid=(nb,),
        in_specs=[pl.BlockSpec((EXPERT_ROWS, D_MODEL), xmap),
                  pl.BlockSpec((None, None, D_MODEL, EXPERT_DIM), wmap),
                  pl.BlockSpec((None, None, D_MODEL, EXPERT_DIM), wmap),
                  pl.BlockSpec((None, None, EXPERT_DIM, D_MODEL), wmap)],
        out_specs=pl.BlockSpec((EXPERT_ROWS, D_MODEL), lambda b, be, nu: (b, 0)),
        scratch_shapes=[pltpu.VMEM((D_MODEL, EXPERT_DIM), BF16),
                        pltpu.VMEM((D_MODEL, EXPERT_DIM), BF16),
                        pltpu.VMEM((EXPERT_DIM, D_MODEL), BF16)],
    )
    return pl.pallas_call(
        _expert_kernel,
        grid_spec=grid_spec,
        out_shape=jax.ShapeDtypeStruct((cap, D_MODEL), F32),
        compiler_params=_params(("arbitrary",)),
        name="moe_experts",
    )(blk_e, n_used, xs, w1, w3, w2)


def _combine_kernel(dest_ref, wts_ref, h_ref, ys_ref, ws1_ref, ws3_ref, ws2_ref, g_ref, b_ref,
                    o_ref, gbuf, sem, *, tm):
    def row_copy(t, k):
        return pltpu.make_async_copy(ys_ref.at[pl.ds(dest_ref[0, 0, t * TOP_K + k], 1)],
                                     gbuf.at[k, pl.ds(t, 1)], sem)

    def start(t, c):
        for k in range(TOP_K):
            row_copy(t, k).start()
        return c

    def wait(t, c):
        for k in range(TOP_K):
            row_copy(t, k).wait()
        return c

    lax.fori_loop(0, tm, start, 0)
    x = h_ref[...]
    xb = x.astype(BF16)
    h1 = _dg(xb, ws1_ref[...], _NN)
    h3 = _dg(xb, ws3_ref[...], _NN)
    acc = _dg((h1 * _sigmoid(h1) * h3).astype(BF16), ws2_ref[...], _NN)
    lax.fori_loop(0, tm, wait, 0)
    w = wts_ref[...]
    for k in range(TOP_K):
        acc = acc + gbuf[k] * w[:, k:k + 1]
    o_ref[...] = _layer_norm(ALPHA * x + acc, g_ref[...], b_ref[...])


def _combine(h, ys, dest, wts, ws1, ws3, ws2, g, b, tm):
    t = h.shape[0]
    dest3 = dest.reshape(t // tm, 1, tm * TOP_K)
    row = pl.BlockSpec((1, D_MODEL), lambda i: (0, 0))
    full = lambda a: pl.BlockSpec(a.shape, lambda i: (0, 0))
    return pl.pallas_call(
        functools.partial(_combine_kernel, tm=tm),
        grid=(t // tm,),
        in_specs=[pl.BlockSpec((1, 1, tm * TOP_K), lambda i: (i, 0, 0), memory_space=pltpu.SMEM),
                  pl.BlockSpec((tm, TOP_K), lambda i: (i, 0)),
                  pl.BlockSpec((tm, D_MODEL), lambda i: (i, 0)),
                  pl.BlockSpec(memory_space=pl.ANY),
                  full(ws1), full(ws3), full(ws2), row, row],
        out_specs=pl.BlockSpec((tm, D_MODEL), lambda i: (i, 0)),
        out_shape=jax.ShapeDtypeStruct((t, D_MODEL), F32),
        scratch_shapes=[pltpu.VMEM((TOP_K, tm, D_MODEL), F32), pltpu.SemaphoreType.DMA(())],
        compiler_params=_params(("arbitrary",)),
        name="moe_combine",
    )(dest3, wts, h, ys, ws1, ws3, ws2, g, b)


def _moe(h, router_w, router_bias, w1, w3, w2, layer, ws1, ws3, ws2, ln_g, ln_b):
    t = h.shape[0]
    eidx, pos, wts, counts = _router(h, router_w.T, router_bias.reshape(N_EXPERTS, 1), tm=min(t, 512))
    counts = counts.reshape(N_EXPERTS)
    pcounts = (counts + EXPERT_ROWS - 1) // EXPERT_ROWS * EXPERT_ROWS
    pends = jnp.cumsum(pcounts)
    pstarts = pends - pcounts
    expert_start = jnp.sum(jnp.where(eidx[None] == jnp.arange(N_EXPERTS)[:, None, None],
                                     pstarts[:, None, None], 0), axis=0)
    dest = (expert_start + pos).T.reshape(-1)
    nb = t * TOP_K // EXPERT_ROWS + N_EXPERTS
    blk_start = jnp.arange(nb, dtype=jnp.int32) * EXPERT_ROWS
    blk_e = jnp.minimum(jnp.sum(blk_start[:, None] >= pends[None, :], axis=1), N_EXPERTS - 1).astype(jnp.int32)
    n_used = (pends[-1] // EXPERT_ROWS).astype(jnp.int32).reshape(1)
    xs = _dispatch(h, dest, jnp.zeros((nb * EXPERT_ROWS, D_MODEL), F32), tm=min(t, 256))
    ys = _experts(xs, blk_e, n_used, w1, w3, w2, layer)
    return _combine(h, ys, dest, wts.T, ws1.astype(BF16), ws3.astype(BF16), ws2.astype(BF16),
                    ln_g.reshape(1, -1), ln_b.reshape(1, -1), tm=min(t, 128))


def _diffattn_kernel(qi_ref, ki_ref, q_ref, k_ref, v_ref, lam_ref, sg_ref, o_ref, qs_ref, m_ref, l_ref, acc_ref,
                     *, tq, lam_init):
    step = pl.program_id(1)
    qi = qi_ref[step]
    ki = ki_ref[step]

    @pl.when(ki == 0)
    def _():
        q = q_ref[...]
        map0 = lax.broadcasted_iota(jnp.int32, q.shape, 1) < DIFF_HEAD_DIM
        zero = jnp.zeros_like(q)
        qs_ref[:tq, :] = jnp.where(map0, q, zero)
        qs_ref[tq:, :] = jnp.where(map0, zero, q)
        m_ref[...] = jnp.full_like(m_ref, -jnp.inf)
        l_ref[...] = jnp.zeros_like(l_ref)
        acc_ref[...] = jnp.zeros_like(acc_ref)

    def accumulate(masked):
        s = _dg(qs_ref[...], k_ref[...], _NT)
        if masked:
            row = lax.broadcasted_iota(jnp.int32, s.shape, 0)
            col = lax.broadcasted_iota(jnp.int32, s.shape, 1)
            s = jnp.where(col <= jnp.where(row >= tq, row - tq, row), s, -jnp.inf)
        chunks = [s[:, c * LANES:(c + 1) * LANES] for c in range(tq // LANES)]
        cmax = chunks[0]
        for x in chunks[1:]:
            cmax = jnp.maximum(cmax, x)
        m_old = m_ref[...]
        m_new = jnp.maximum(m_old, jnp.max(cmax, axis=-1, keepdims=True))
        alpha = jnp.exp2(m_old - m_new)
        ps = [jnp.exp2(x - m_new) for x in chunks]
        psum = ps[0]
        for x in ps[1:]:
            psum = psum + x
        l_ref[...] = alpha * l_ref[...] + psum
        p = jnp.concatenate([x.astype(BF16) for x in ps], axis=-1)
        acc_ref[...] = alpha * acc_ref[...] + _dg(p, v_ref[...], _NN)
        m_ref[...] = m_new

    @pl.when(ki < qi)
    def _():
        accumulate(False)

    @pl.when(ki == qi)
    def _():
        accumulate(True)
        lp = lam_ref[...]
        lam = (jnp.exp(jnp.sum(lp[0:1] * lp[1:2], axis=-1, keepdims=True))
               - jnp.exp(jnp.sum(lp[2:3] * lp[3:4], axis=-1, keepdims=True)) + lam_init)
        l = jnp.sum(l_ref[...], axis=-1, keepdims=True)
        o = acc_ref[:tq, :] / l[:tq, :] - lam * (acc_ref[tq:, :] / l[tq:, :])
        o = o * lax.rsqrt(jnp.mean(o * o, axis=-1, keepdims=True) + RMS_EPS)
        o_ref[...] = o * sg_ref[...] * (1.0 - lam_init)


def _diffattn(q, kv, lam_params, subln_g, lam_init, tq):
    t = q.shape[0]
    nh = DIFF_HEADS
    nq = t // tq
    pairs = [(a, b) for a in range(nq) for b in range(a + 1)]
    qi_tab = jnp.asarray([a for a, _ in pairs], jnp.int32)
    ki_tab = jnp.asarray([b for _, b in pairs], jnp.int32)
    grid_spec = pltpu.PrefetchScalarGridSpec(
        num_scalar_prefetch=2,
        grid=(nh, len(pairs)),
        in_specs=[pl.BlockSpec((tq, LANES), lambda h, s, qt, kt: (qt[s], h)),
                  pl.BlockSpec((tq, LANES), lambda h, s, qt, kt: (kt[s], h)),
                  pl.BlockSpec((tq, LANES), lambda h, s, qt, kt: (kt[s], nh + h)),
                  pl.BlockSpec(lam_params.shape, lambda h, s, qt, kt: (0, 0)),
                  pl.BlockSpec((1, LANES), lambda h, s, qt, kt: (0, 0))],
        out_specs=pl.BlockSpec((tq, LANES), lambda h, s, qt, kt: (qt[s], h)),
        scratch_shapes=[pltpu.VMEM((2 * tq, LANES), BF16),
                        pltpu.VMEM((2 * tq, LANES), F32),
                        pltpu.VMEM((2 * tq, LANES), F32),
                        pltpu.VMEM((2 * tq, LANES), F32)],
    )
    return pl.pallas_call(
        functools.partial(_diffattn_kernel, tq=tq, lam_init=lam_init),
        grid_spec=grid_spec,
        out_shape=jax.ShapeDtypeStruct((t, MAIN_WIDTH), F32),
        compiler_params=_params(("parallel", "arbitrary")),
        name="diff_attn",
    )(qi_tab, ki_tab, q, kv, kv, lam_params, subln_g)


def _pad_rows(w, rows):
    return jnp.pad(w, ((0, rows - w.shape[0]), (0, 0)))


def _layer_a(h, memkv, a_w_in, a_mu, a_w0, a_w_decay_up, a_a0, a_w_a_up, a_w_g_up, a_k_k, a_k_a,
             a_r_k, a_gn_g, a_gn_b, w_out, ln1_g, ln1_b, tm):
    c = MAIN_WIDTH
    lo = 3 * c

    def pad_cols(m):
        pieces = [m[..., :lo],
                  jnp.pad(m[..., lo:lo + DECAY_LORA], ((0, 0), (0, LANES - DECAY_LORA))),
                  jnp.pad(m[..., lo + DECAY_LORA:lo + DECAY_LORA + AAA_LORA], ((0, 0), (0, LANES - AAA_LORA))),
                  jnp.pad(m[..., lo + DECAY_LORA + AAA_LORA:lo + DECAY_LORA + AAA_LORA + GATE_LORA],
                          ((0, 0), (0, 2 * LANES - GATE_LORA))),
                  m[..., lo + DECAY_LORA + AAA_LORA + GATE_LORA:]]
        return jnp.concatenate(pieces, axis=-1)

    w_in = pad_cols(a_w_in).astype(BF16)
    mu_pad = pad_cols(jnp.concatenate([a_mu, jnp.zeros((MEM_WIDTH,), F32)]).reshape(1, -1))[:, :lo + LORA_TILE]
    z = _proj(h, w_in, tm=tm, tn=512, out_dtype=F32)
    lane = jnp.arange(LANES)
    g128 = (lane[:, None] // RWKV_HEAD_DIM == lane[None, :] // RWKV_HEAD_DIM).astype(BF16)
    row = lambda a: a.reshape(1, -1)
    r, lw, k, v, an, b, g = _rwkv_prep(
        z, mu_pad, row(a_w0), _pad_rows(a_w_decay_up, LANES), row(a_a0), _pad_rows(a_w_a_up, LANES),
        _pad_rows(a_w_g_up, 2 * LANES), row(a_k_k), row(a_k_a), g128, tm=min(tm, 256))
    main = _wkv(r, lw, k, v, an, b, g, row(a_r_k), row(a_gn_g), row(a_gn_b), g128)
    memo = _memattn(z, (lo + LORA_TILE) // MEM_WIDTH, memkv, tm=min(tm, 512))
    return _outproj_ln(main, memo, h, w_out.astype(BF16), row(ln1_g), row(ln1_b), tm=min(tm, 256))


def _layer_b(h, memkv, positions, kv_shared_w, b_w_in, lam_params, subln_g, lam_init, w_out, ln1_g, ln1_b, tm):
    c = MAIN_WIDTH
    rope_k = _rope_tables(positions, tm=min(tm, 512), scale=1.0)
    rope_q = _rope_tables(positions, tm=min(tm, 512), scale=DIFF_HEAD_DIM ** -0.5 * math.log2(math.e))
    kv = _proj(h, kv_shared_w.astype(BF16), tm=tm, tn=512, out_dtype=BF16, rope=rope_k, n_rope=c // 512)
    zq = _proj(h, b_w_in.astype(BF16), tm=tm, tn=512, out_dtype=BF16, rope=rope_q, n_rope=c // 512)
    main = _diffattn(zq, kv, lam_params, subln_g.reshape(1, -1), lam_init, tq=min(tm, ATTN_TILE))
    memo = _memattn(zq, c // MEM_WIDTH, memkv, tm=min(tm, 512))
    row = lambda a: a.reshape(1, -1)
    return _outproj_ln(main, memo, h, w_out.astype(BF16), row(ln1_g), row(ln1_b), tm=min(tm, 256))


def kernel(x, mem, positions, a_w_in, a_mu, a_w0, a_w_decay_up, a_a0, a_w_a_up, a_w_g_up, a_k_k, a_k_a, a_r_k, a_gn_g, a_gn_b, kv_shared_w, b_w_in, b_lambda_q1, b_lambda_k1, b_lambda_q2, b_lambda_k2, b_subln_g, mem_w_kv, w_out, ln1_g, ln1_b, ln2_g, ln2_b, router_w, router_bias, exp_w1, exp_w3, exp_w2, sh_w1, sh_w3, sh_w2):
    bsz, t, _ = x.shape
    assert bsz == 1
    h = x.reshape(t, D_MODEL)
    tm = min(t, 1024)
    mem2 = mem.reshape(-1, D_MODEL)
    n_a = DEPTH // 2
    for l in range(DEPTH):
        memkv = _proj(mem2, mem_w_kv[l].astype(BF16), tm=mem2.shape[0], tn=512, out_dtype=F32)
        if l < n_a:
            h = _layer_a(h, memkv, a_w_in[l], a_mu[l], a_w0[l], a_w_decay_up[l], a_a0[l], a_w_a_up[l],
                         a_w_g_up[l], a_k_k[l], a_k_a[l], a_r_k[l], a_gn_g[l], a_gn_b[l],
                         w_out[l], ln1_g[l], ln1_b[l], tm)
        else:
            j = l - n_a
            lam_init = 0.8 - 0.6 * math.exp(-0.3 * l)
            lam_params = jnp.stack([b_lambda_q1[j], b_lambda_k1[j], b_lambda_q2[j], b_lambda_k2[j]])
            h = _layer_b(h, memkv, positions, kv_shared_w, b_w_in[j], lam_params, b_subln_g[j], lam_init,
                         w_out[l], ln1_g[l], ln1_b[l], tm)
        h = _moe(h, router_w[l], router_bias[l], exp_w1, exp_w3, exp_w2, l,
                 sh_w1[l], sh_w3[l], sh_w2[l], ln2_g[l], ln2_b[l])
    return h.reshape(bsz, t, D_MODEL)
```

```python
import functools
import math

import jax
import jax.numpy as jnp
from jax import lax
from jax.experimental import pallas as pl
from jax.experimental.pallas import tpu as pltpu

F32 = jnp.float32
BF16 = jnp.bfloat16

D_MODEL = 2048
DEPTH = 2
MEM_HEADS = 4
MEM_WIDTH = D_MODEL // 4
MEM_HEAD_DIM = MEM_WIDTH // MEM_HEADS
MAIN_WIDTH = D_MODEL - MEM_WIDTH
RWKV_HEAD_DIM = 64
DECAY_LORA = 64
AAA_LORA = 64
GATE_LORA = 224
GN_EPS = 64e-5
DIFF_HEADS = 12
DIFF_HEAD_DIM = 64
ROPE_DIM = DIFF_HEAD_DIM // 4
ROPE_THETA = 500000.0
N_EXPERTS = 64
TOP_K = 8
N_GROUPS = 8
GROUP_SIZE = N_EXPERTS // N_GROUPS
TOPK_GROUPS = 4
EXPERT_DIM = D_MODEL // 4
ROUTED_SCALE = 2.5
ALPHA = (2 * DEPTH) ** 0.25
LN_EPS = 1e-5
RMS_EPS = 1e-5

LANES = 128
VMEM_LIMIT = 56 * 1024 * 1024

LORA_TILE = 512
A_IN_PAD = 3 * MAIN_WIDTH + LORA_TILE + MEM_WIDTH
WKV_CHUNK = 64
WKV_PAIRS = 12
EXPERT_ROWS = 256
ATTN_TILE = 512


def _params(sem):
    return pltpu.CompilerParams(dimension_semantics=sem, vmem_limit_bytes=VMEM_LIMIT)


def _split2(x):
    hi = x.astype(BF16)
    lo = (x - hi.astype(F32)).astype(BF16)
    return hi, lo


def _dg(a, b, dims):
    return lax.dot_general(a, b, (dims, ((), ())), preferred_element_type=F32)


_NN = ((1,), (0,))
_NT = ((1,), (1,))
_TN = ((0,), (0,))


def _dot3(a, b, dims=_NN):
    ah, al = _split2(a)
    bh, bl = _split2(b)
    return _dg(ah, bh, dims) + (_dg(ah, bl, dims) + _dg(al, bh, dims))


def _dot2l(a, b_exact, dims=_NN):
    ah, al = _split2(a)
    return _dg(ah, b_exact, dims) + _dg(al, b_exact, dims)


def _dot1(a, b, dims=_NN):
    return _dg(a.astype(BF16), b.astype(BF16), dims)


def _sigmoid(x):
    return 1.0 / (1.0 + jnp.exp(-x))


def _proj_kernel(*refs, n_rope, tn, cast_x):
    if n_rope:
        x_ref, w_ref, c_ref, sa_ref, sb_ref, o_ref = refs[:6]
        rest = refs[6:]
    else:
        x_ref, w_ref, o_ref = refs[:3]
        rest = refs[3:]
    j = pl.program_id(1)
    if cast_x:
        xb_ref, = rest

        @pl.when(j == 0)
        def _():
            xb_ref[...] = x_ref[...].astype(BF16)
        xb = xb_ref[...]
    else:
        xb = x_ref[...]
    acc = jnp.dot(xb, w_ref[...], preferred_element_type=F32)
    if n_rope:
        @pl.when(j < n_rope)
        def _():
            c, sa, sb = c_ref[...], sa_ref[...], sb_ref[...]
            for q in range(tn // LANES):
                a = acc[:, q * LANES:(q + 1) * LANES]
                rot = a * c + pltpu.roll(a, LANES - ROPE_DIM // 2, 1) * sa + pltpu.roll(a, ROPE_DIM // 2, 1) * sb
                o_ref[:, q * LANES:(q + 1) * LANES] = rot.astype(o_ref.dtype)

        @pl.when(j >= n_rope)
        def _():
            o_ref[...] = acc.astype(o_ref.dtype)
    else:
        o_ref[...] = acc.astype(o_ref.dtype)


def _proj(x, w, *, tm, tn, out_dtype, rope=None, n_rope=0):
    m, k = x.shape
    n = w.shape[1]
    assert m % tm == 0 and n % tn == 0
    cast_x = x.dtype != BF16
    in_specs = [pl.BlockSpec((tm, k), lambda i, j: (i, 0)),
                pl.BlockSpec((k, tn), lambda i, j: (0, j))]
    args = [x, w]
    if n_rope:
        in_specs += [pl.BlockSpec((tm, LANES), lambda i, j: (i, 0))] * 3
        args += list(rope)
    return pl.pallas_call(
        functools.partial(_proj_kernel, n_rope=n_rope, tn=tn, cast_x=cast_x),
        grid=(m // tm, n // tn),
        in_specs=in_specs,
        out_specs=pl.BlockSpec((tm, tn), lambda i, j: (i, j)),
        out_shape=jax.ShapeDtypeStruct((m, n), out_dtype),
        scratch_shapes=[pltpu.VMEM((tm, k), BF16)] if cast_x else [],
        compiler_params=_params(("parallel", "arbitrary")),
        name="proj",
    )(*args)


def _rope_table_kernel(pos_ref, invf_ref, c_ref, sa_ref, sb_ref, *, scale):
    ang = pos_ref[...].astype(F32) * invf_ref[...]
    lane = lax.broadcasted_iota(jnp.int32, ang.shape, 1) & (DIFF_HEAD_DIM - 1)
    c = jnp.cos(ang)
    s = jnp.sin(ang)
    half = ROPE_DIM // 2
    c_ref[...] = jnp.where(lane < ROPE_DIM, c, 1.0) * scale
    sa_ref[...] = jnp.where(lane < half, -s, 0.0) * scale
    sb_ref[...] = jnp.where(lane < half, 0.0, jnp.where(lane < ROPE_DIM, s, 0.0)) * scale


def _rope_tables(positions, tm, scale):
    t = positions.shape[1]
    half = ROPE_DIM // 2
    inv_freq = ROPE_THETA ** (-jnp.arange(half, dtype=F32) / half)
    lane = jnp.arange(LANES) % DIFF_HEAD_DIM
    invf = jnp.where(lane < ROPE_DIM, inv_freq[lane % half], 0.0).reshape(1, LANES).astype(F32)
    pos = positions.reshape(t, 1)
    shp = jax.ShapeDtypeStruct((t, LANES), F32)
    return pl.pallas_call(
        functools.partial(_rope_table_kernel, scale=scale),
        grid=(t // tm,),
        in_specs=[pl.BlockSpec((tm, 1), lambda i: (i, 0)), pl.BlockSpec((1, LANES), lambda i: (0, 0))],
        out_specs=[pl.BlockSpec((tm, LANES), lambda i: (i, 0))] * 3,
        out_shape=[shp, shp, shp],
        compiler_params=_params(("parallel",)),
        name="rope_tables",
    )(pos, invf)


def _head_sum(x, g128):
    return _dot2l(x, g128)


def _prep_kernel(z_ref, mu_ref, w0_ref, wdu_ref, a0_ref, wau_ref, wgu_ref, kk_ref, ka_ref, g128_ref,
                 r_o, lw_o, k_o, v_o, an_o, b_o, g_o, carry_ref, *, tm):
    i = pl.program_id(0)

    @pl.when(i == 0)
    def _():
        carry_ref[...] = jnp.zeros_like(carry_ref)

    row0 = lax.broadcasted_iota(jnp.int32, (tm, 1), 0) == 0
    c = MAIN_WIDTH

    def shifted(c0, c1):
        z = z_ref[:, c0:c1]
        prev = jnp.where(row0, carry_ref[:, c0:c1], pltpu.roll(z, 1, 0))
        return z + (prev - z) * mu_ref[:, c0:c1]

    r = shifted(0, c)
    k = shifted(c, 2 * c)
    v = shifted(2 * c, 3 * c)
    lora = shifted(3 * c, 3 * c + LORA_TILE)
    carry_ref[...] = z_ref[pl.ds(tm - 1, 1), :]

    dw = lora[:, 0:LANES]
    da = lora[:, LANES:2 * LANES]
    dg = lora[:, 2 * LANES:4 * LANES]
    wl = w0_ref[...] + _dot3(jnp.tanh(dw), wdu_ref[...])
    lw = -_sigmoid(wl) * math.exp(-0.5)
    a = _sigmoid(a0_ref[...] + _dot3(da, wau_ref[...]))
    g = _dot3(_sigmoid(dg), wgu_ref[...])
    kk = k * kk_ref[...]
    g128 = g128_ref[...]
    for q in range(c // LANES):
        sl = slice(q * LANES, (q + 1) * LANES)
        kq = kk[:, sl]
        nrm = jnp.maximum(jnp.sqrt(_head_sum(kq * kq, g128)), 1e-12)
        kn = kq / nrm
        an_o[:, sl] = -kn
        b_o[:, sl] = kn * a[:, sl]
    r_o[...] = r
    lw_o[...] = lw
    k_o[...] = k * (1.0 + (a - 1.0) * ka_ref[...])
    v_o[...] = v
    g_o[...] = g


def _rwkv_prep(z, mu_pad, w0, wdu_pad, a0, wau_pad, wgu_pad, k_k, k_a, g128, tm):
    t = z.shape[0]
    c = MAIN_WIDTH
    wz = 3 * c + LORA_TILE
    row = lambda n: pl.BlockSpec((1, n), lambda i: (0, 0))
    full = lambda a: pl.BlockSpec(a.shape, lambda i: (0, 0))
    shp = jax.ShapeDtypeStruct((t, c), F32)
    return pl.pallas_call(
        functools.partial(_prep_kernel, tm=tm),
        grid=(t // tm,),
        in_specs=[pl.BlockSpec((tm, wz), lambda i: (i, 0)), row(wz), row(c), full(wdu_pad), row(c),
                  full(wau_pad), full(wgu_pad), row(c), row(c), full(g128)],
        out_specs=[pl.BlockSpec((tm, c), lambda i: (i, 0))] * 7,
        out_shape=[shp] * 7,
        scratch_shapes=[pltpu.VMEM((1, wz), F32)],
        compiler_params=_params(("arbitrary",)),
        name="rwkv_prep",
    )(z, mu_pad, w0, wdu_pad, a0, wau_pad, wgu_pad, k_k, k_a, g128)


def _wkv_kernel(r_ref, lw_ref, k_ref, v_ref, an_ref, b_ref, g_ref, rk_ref, gng_ref, gnb_ref, g128_ref,
                o_ref, ht_ref, *, L, pairs):
    ci = pl.program_id(1)

    @pl.when(ci == 0)
    def _():
        ht_ref[...] = jnp.zeros_like(ht_ref)

    P = range(pairs)
    cols = [slice(p * LANES, (p + 1) * LANES) for p in P]
    ld = lambda ref: [ref[:, c] for c in cols]
    r, lw, k, v, an, b = ld(r_ref), ld(lw_ref), ld(k_ref), ld(v_ref), ld(an_ref), ld(b_ref)
    g128 = g128_ref[...]
    n2 = 2 * L

    rowi = lax.broadcasted_iota(jnp.int32, (L, L), 0)
    coli = lax.broadcasted_iota(jnp.int32, (L, L), 1)
    tri = (coli <= rowi).astype(BF16)

    def running_sum(x):
        x_h = x.astype(BF16)
        x_r = x - x_h.astype(F32)
        x_m = x_r.astype(BF16)
        x_l = (x_r - x_m.astype(F32)).astype(BF16)
        return _dg(tri, x_h, _NN) + (_dg(tri, x_m, _NN) + _dg(tri, x_l, _NN))

    cum = [running_sum(lw[p]) for p in P]
    cum_last = [cum[p][L - 1:L, :] for p in P]
    p_inv = [jnp.exp(-cum[p]) for p in P]
    at = [an[p] * jnp.exp(cum[p] - lw[p]) for p in P]
    rt = [r[p] * jnp.exp(cum[p]) for p in P]
    bt = [b[p] * p_inv[p] for p in P]
    kt = [k[p] * p_inv[p] for p in P]
    p_end = [jnp.exp(cum_last[p] - cum[p]) for p in P]

    head0 = lax.broadcasted_iota(jnp.int32, (L, LANES), 1) < RWKV_HEAD_DIM

    def stack(x):
        return jnp.concatenate([jnp.where(head0, x, 0.0), jnp.where(head0, 0.0, x)], axis=0)

    ar = [jnp.concatenate([stack(at[p]), stack(rt[p])], axis=0) for p in P]
    bk = [jnp.concatenate([stack(bt[p]), stack(kt[p])], axis=0) for p in P]
    v_st = [stack(v[p]) for p in P]
    m = [_dot3(ar[p], bk[p], _NT) for p in P]

    ri = lax.broadcasted_iota(jnp.int32, (n2, n2), 0)
    cj = lax.broadcasted_iota(jnp.int32, (n2, n2), 1)
    same = (ri >= L) == (cj >= L)
    strict = same & (cj < ri)
    incl = same & (cj <= ri)
    ab = [jnp.where(strict, m[p][:n2, :n2], 0.0) for p in P]
    akrk = [jnp.concatenate([jnp.where(strict, m[p][:n2, n2:], 0.0),
                             jnp.where(incl, m[p][n2:, n2:], 0.0)], axis=0) for p in P]
    rb = [jnp.where(incl, m[p][n2:, :n2], 0.0) for p in P]

    ht = [ht_ref[p] for p in P]
    arh = [_dot3(ar[p], ht[p], _NT) for p in P]
    kv = [_dot3(akrk[p], v_st[p]) for p in P]
    u = [arh[p][:n2] + kv[p][:n2] for p in P]
    apow = ab
    steps = max(1, (L - 1).bit_length())
    for s in range(steps):
        u = [u[p] + _dot1(apow[p], u[p]) for p in P]
        if s + 1 < steps:
            apow = [_dot1(apow[p], apow[p]) for p in P]
    y_st = [arh[p][n2:] + kv[p][n2:] + _dot1(rb[p], u[p]) for p in P]
    y = [y_st[p][:L] + y_st[p][L:] for p in P]

    bk_end = [jnp.concatenate([stack(b[p] * p_end[p]), stack(k[p] * p_end[p])], axis=0) for p in P]
    for p in P:
        uv = jnp.concatenate([u[p], v_st[p]], axis=0)
        ht_ref[p] = ht[p] * jnp.exp(cum_last[p]) + _dot3(uv, bk_end[p], _TN)

    inv_n = 1.0 / RWKV_HEAD_DIM
    mu = [_head_sum(y[p], g128) * inv_n for p in P]
    yc = [y[p] - mu[p] for p in P]
    var = [_head_sum(yc[p] * yc[p], g128) * inv_n for p in P]
    bonus = [_head_sum(r[p] * k[p] * rk_ref[:, cols[p]], g128) * v[p] for p in P]
    for p in P:
        yn = yc[p] * lax.rsqrt(var[p] + GN_EPS) * gng_ref[:, cols[p]] + gnb_ref[:, cols[p]]
        o_ref[:, cols[p]] = (yn + bonus[p]) * g_ref[:, cols[p]]


def _wkv(r, lw, k, v, an, b, g, r_k, gn_g, gn_b, g128):
    t, c = r.shape
    L = WKV_CHUNK
    w = WKV_PAIRS * LANES
    tok = pl.BlockSpec((L, w), lambda p, ci: (ci, p))
    par = pl.BlockSpec((1, w), lambda p, ci: (0, p))
    return pl.pallas_call(
        functools.partial(_wkv_kernel, L=L, pairs=WKV_PAIRS),
        grid=(c // w, t // L),
        in_specs=[tok] * 7 + [par] * 3 + [pl.BlockSpec((LANES, LANES), lambda p, ci: (0, 0))],
        out_specs=tok,
        out_shape=jax.ShapeDtypeStruct((t, c), F32),
        scratch_shapes=[pltpu.VMEM((WKV_PAIRS, LANES, LANES), F32)],
        compiler_params=_params(("parallel", "arbitrary")),
        name="wkv7",
    )(r, lw, k, v, an, b, g, r_k, gn_g, gn_b, g128)


def _memattn_kernel(q_ref, kv_ref, o_ref):
    scale = MEM_HEAD_DIM ** -0.5
    for hh in range(MEM_HEADS):
        sl = slice(hh * MEM_HEAD_DIM, (hh + 1) * MEM_HEAD_DIM)
        q = q_ref[:, sl].astype(BF16)
        kh = kv_ref[:, sl].astype(BF16)
        vh = kv_ref[:, MEM_WIDTH + hh * MEM_HEAD_DIM:MEM_WIDTH + (hh + 1) * MEM_HEAD_DIM].astype(BF16)
        s = _dg(q, kh, _NT) * scale
        s = s - jnp.max(s, axis=-1, keepdims=True)
        e = jnp.exp(s)
        p = e / jnp.sum(e, axis=-1, keepdims=True)
        o_ref[:, sl] = _dg(p.astype(BF16), vh, _NN)


def _memattn(zq, q_col_block, memkv, tm):
    t = zq.shape[0]
    return pl.pallas_call(
        _memattn_kernel,
        grid=(t // tm,),
        in_specs=[pl.BlockSpec((tm, MEM_WIDTH), lambda i: (i, q_col_block)),
                  pl.BlockSpec(memkv.shape, lambda i: (0, 0))],
        out_specs=pl.BlockSpec((tm, MEM_WIDTH), lambda i: (i, 0)),
        out_shape=jax.ShapeDtypeStruct((t, MEM_WIDTH), F32),
        compiler_params=_params(("parallel",)),
        name="mem_attn",
    )(zq, memkv)


def _layer_norm(x, g, b):
    mu = jnp.mean(x, axis=-1, keepdims=True)
    xc = x - mu
    var = jnp.mean(xc * xc, axis=-1, keepdims=True)
    return xc * lax.rsqrt(var + LN_EPS) * g + b


HALF = D_MODEL // 2


def _pack_rows(x):
    lo = pltpu.bitcast(x[:, :HALF].astype(BF16).astype(F32), jnp.uint32)
    hi = pltpu.bitcast(x[:, HALF:].astype(BF16).astype(F32), jnp.uint32)
    return hi | lax.shift_right_logical(lo, jnp.uint32(16))


def _unpack_rows(p):
    lo = pltpu.bitcast(lax.shift_left(p, jnp.uint32(16)), F32).astype(BF16)
    hi = pltpu.bitcast(p & jnp.uint32(0xFFFF0000), F32).astype(BF16)
    return lo, hi


def _outproj_kernel(main_ref, memo_ref, h_ref, w_ref, g_ref, b_ref, o_ref, op_ref):
    mix = (_dg(main_ref[...].astype(BF16), w_ref[:MAIN_WIDTH, :], _NN)
           + _dg(memo_ref[...].astype(BF16), w_ref[MAIN_WIDTH:, :], _NN))
    y = _layer_norm(ALPHA * h_ref[...] + mix, g_ref[...], b_ref[...])
    o_ref[...] = y
    op_ref[...] = _pack_rows(y)


def _outproj_ln(main, memo, h, w_bf, g, b, tm):
    t = h.shape[0]
    row = pl.BlockSpec((1, D_MODEL), lambda i: (0, 0))
    return pl.pallas_call(
        _outproj_kernel,
        grid=(t // tm,),
        in_specs=[pl.BlockSpec((tm, MAIN_WIDTH), lambda i: (i, 0)),
                  pl.BlockSpec((tm, MEM_WIDTH), lambda i: (i, 0)),
                  pl.BlockSpec((tm, D_MODEL), lambda i: (i, 0)),
                  pl.BlockSpec((D_MODEL, D_MODEL), lambda i: (0, 0)), row, row],
        out_specs=[pl.BlockSpec((tm, D_MODEL), lambda i: (i, 0)),
                   pl.BlockSpec((tm, HALF), lambda i: (i, 0))],
        out_shape=[jax.ShapeDtypeStruct((t, D_MODEL), F32),
                   jax.ShapeDtypeStruct((t, HALF), jnp.uint32)],
        compiler_params=_params(("parallel",)),
        name="outproj_ln",
    )(main, memo, h, w_bf, g, b)


def _first_max(vals, ids):
    m = vals[0]
    for x in vals[1:]:
        m = jnp.maximum(m, x)
    m = jnp.max(m, axis=0, keepdims=True)
    big = jnp.int32(1 << 20)
    idx = None
    for x, e in zip(vals, ids):
        cand = jnp.where(x == m, e, big)
        idx = cand if idx is None else jnp.minimum(idx, cand)
    idx = jnp.min(idx, axis=0, keepdims=True)
    return m, idx


def _router_kernel(h_ref, wt_ref, bias_ref, eidx_ref, pos_ref, wts_ref, cnt_ref, run_ref, *, tm):
    i = pl.program_id(0)

    @pl.when(i == 0)
    def _():
        run_ref[...] = jnp.zeros_like(run_ref)

    neg = -jnp.inf
    logits = _dot3(wt_ref[...], h_ref[...], _NT)
    scores = _sigmoid(logits)
    biased = scores + bias_ref[...]
    sub = lax.broadcasted_iota(jnp.int32, (GROUP_SIZE, tm), 0)
    grp = [biased[g * GROUP_SIZE:(g + 1) * GROUP_SIZE, :] for g in range(N_GROUPS)]
    sc = [scores[g * GROUP_SIZE:(g + 1) * GROUP_SIZE, :] for g in range(N_GROUPS)]
    ids = [sub + g * GROUP_SIZE for g in range(N_GROUPS)]

    gs = []
    for g in range(N_GROUPS):
        m1, i1 = _first_max([grp[g]], [sub])
        m2 = jnp.max(jnp.where(sub == i1, neg, grp[g]), axis=0, keepdims=True)
        gs.append(m1 + m2)
    gscore = jnp.concatenate(gs, axis=0)
    gmask = jnp.zeros((N_GROUPS, tm), jnp.bool_)
    cur = gscore
    for _ in range(TOPK_GROUPS):
        _, gi = _first_max([cur], [sub])
        hit = sub == gi
        gmask = gmask | hit
        cur = jnp.where(hit, neg, cur)
    gm = gmask.astype(F32)
    cur = [jnp.where(gm[g:g + 1, :] > 0.0, grp[g], neg) for g in range(N_GROUPS)]

    sel = [jnp.zeros((GROUP_SIZE, tm), jnp.bool_) for _ in range(N_GROUPS)]
    e_rows, w_rows = [], []
    for _ in range(TOP_K):
        _, ei = _first_max(cur, ids)
        w = jnp.zeros((1, tm), F32)
        for g in range(N_GROUPS):
            hit = ids[g] == ei
            sel[g] = sel[g] | hit
            cur[g] = jnp.where(hit, neg, cur[g])
            w = w + jnp.sum(jnp.where(hit, sc[g], 0.0), axis=0, keepdims=True)
        e_rows.append(ei)
        w_rows.append(w)
    eidx = jnp.concatenate(e_rows, axis=0)
    wraw = jnp.concatenate(w_rows, axis=0)
    wts_ref[...] = wraw / (jnp.sum(wraw, axis=0, keepdims=True) + 1e-20) * ROUTED_SCALE
    eidx_ref[...] = eidx

    self32 = jnp.concatenate([s.astype(F32) for s in sel], axis=0)
    ui = lax.broadcasted_iota(jnp.int32, (tm, tm), 0)
    uj = lax.broadcasted_iota(jnp.int32, (tm, tm), 1)
    before = (ui < uj).astype(BF16)
    local = _dg(self32.astype(BF16), before, _NN)
    run = run_ref[...]
    glob = local + run
    run_new = run + jnp.sum(self32, axis=1, keepdims=True)
    run_ref[...] = run_new
    cnt_ref[...] = run_new.astype(jnp.int32)
    e64 = lax.broadcasted_iota(jnp.int32, (N_EXPERTS, tm), 0)
    p_rows = [jnp.sum(jnp.where(e64 == e_rows[k], glob, 0.0), axis=0, keepdims=True) for k in range(TOP_K)]
    pos_ref[...] = jnp.concatenate(p_rows, axis=0).astype(jnp.int32)


def _router(h, w_router_t, bias_col, tm):
    t = h.shape[0]
    return pl.pallas_call(
        functools.partial(_router_kernel, tm=tm),
        grid=(t // tm,),
        in_specs=[pl.BlockSpec((tm, D_MODEL), lambda i: (i, 0)),
                  pl.BlockSpec((N_EXPERTS, D_MODEL), lambda i: (0, 0)),
                  pl.BlockSpec((N_EXPERTS, 1), lambda i: (0, 0))],
        out_specs=[pl.BlockSpec((TOP_K, tm), lambda i: (0, i)),
                   pl.BlockSpec((TOP_K, tm), lambda i: (0, i)),
                   pl.BlockSpec((TOP_K, tm), lambda i: (0, i)),
                   pl.BlockSpec((N_EXPERTS, 1), lambda i: (0, 0))],
        out_shape=[jax.ShapeDtypeStruct((TOP_K, t), jnp.int32),
                   jax.ShapeDtypeStruct((TOP_K, t), jnp.int32),
                   jax.ShapeDtypeStruct((TOP_K, t), F32),
                   jax.ShapeDtypeStruct((N_EXPERTS, 1), jnp.int32)],
        scratch_shapes=[pltpu.VMEM((N_EXPERTS, 1), F32)],
        compiler_params=_params(("arbitrary",)),
        name="router",
    )(h, w_router_t, bias_col)


def _dispatch_kernel(dest_ref, h_ref, xs_in_ref, xs_ref, sem, *, tm):
    del xs_in_ref

    def row_copy(t, k):
        return pltpu.make_async_copy(h_ref.at[pl.ds(t, 1)],
                                     xs_ref.at[pl.ds(dest_ref[0, 0, t * TOP_K + k], 1)], sem)

    def start(t, c):
        for k in range(TOP_K):
            row_copy(t, k).start()
        return c

    def wait(t, c):
        for k in range(TOP_K):
            row_copy(t, k).wait()
        return c

    lax.fori_loop(0, tm, start, 0)
    lax.fori_loop(0, tm, wait, 0)


def _dispatch(h, dest, xs_init, tm):
    t = h.shape[0]
    dest3 = dest.reshape(t // tm, 1, tm * TOP_K)
    return pl.pallas_call(
        functools.partial(_dispatch_kernel, tm=tm),
        grid=(t // tm,),
        in_specs=[pl.BlockSpec((1, 1, tm * TOP_K), lambda i: (i, 0, 0), memory_space=pltpu.SMEM),
                  pl.BlockSpec((tm, h.shape[1]), lambda i: (i, 0)),
                  pl.BlockSpec(memory_space=pl.ANY)],
        out_specs=pl.BlockSpec(memory_space=pl.ANY),
        out_shape=jax.ShapeDtypeStruct(xs_init.shape, xs_init.dtype),
        scratch_shapes=[pltpu.SemaphoreType.DMA(())],
        input_output_aliases={2: 0},
        compiler_params=_params(("arbitrary",)),
        name="moe_dispatch",
    )(dest3, h, xs_init)


def _expert_kernel(blk_e_ref, n_used_ref, x_ref, w1_ref, w3_ref, w2_ref, y_ref, w1b, w3b, w2b):
    b = pl.program_id(0)
    used = b < n_used_ref[0]
    prev = blk_e_ref[jnp.maximum(b - 1, 0)]
    fresh = jnp.logical_or(b == 0, blk_e_ref[b] != prev)

    @pl.when(jnp.logical_and(used, fresh))
    def _():
        w1b[...] = w1_ref[...].astype(BF16)
        w3b[...] = w3_ref[...].astype(BF16)
        w2b[...] = w2_ref[...].astype(BF16)

    @pl.when(used)
    def _():
        x_lo, x_hi = _unpack_rows(x_ref[...])
        h1 = _dg(x_lo, w1b[:HALF, :], _NN) + _dg(x_hi, w1b[HALF:, :], _NN)
        h3 = _dg(x_lo, w3b[:HALF, :], _NN) + _dg(x_hi, w3b[HALF:, :], _NN)
        hb = (h1 * _sigmoid(h1) * h3).astype(BF16)
        y_ref[...] = _pack_rows(_dg(hb, w2b[...], _NN))

    @pl.when(jnp.logical_not(used))
    def _():
        y_ref[...] = jnp.zeros_like(y_ref)


def _experts(xs, blk_e, n_used, w1, w3, w2, layer):
    cap = xs.shape[0]
    nb = cap // EXPERT_ROWS
    xmap = lambda b, be, nu: (jnp.minimum(b, nu[0] - 1), 0)
    wmap = lambda b, be, nu: (layer, be[b], 0, 0)
    grid_spec = pltpu.PrefetchScalarGridSpec(
        num_scalar_prefetch=2,
        grid=(nb,),
        in_specs=[pl.BlockSpec((EXPERT_ROWS, HALF), xmap),
                  pl.BlockSpec((None, None, D_MODEL, EXPERT_DIM), wmap),
                  pl.BlockSpec((None, None, D_MODEL, EXPERT_DIM), wmap),
                  pl.BlockSpec((None, None, EXPERT_DIM, D_MODEL), wmap)],
        out_specs=pl.BlockSpec((EXPERT_ROWS, HALF), lambda b, be, nu: (b, 0)),
        scratch_shapes=[pltpu.VMEM((D_MODEL, EXPERT_DIM), BF16),
                        pltpu.VMEM((D_MODEL, EXPERT_DIM), BF16),
                        pltpu.VMEM((EXPERT_DIM, D_MODEL), BF16)],
    )
    return pl.pallas_call(
        _expert_kernel,
        grid_spec=grid_spec,
        out_shape=jax.ShapeDtypeStruct((cap, HALF), jnp.uint32),
        compiler_params=_params(("arbitrary",)),
        name="moe_experts",
    )(blk_e, n_used, xs, w1, w3, w2)


def _combine_kernel(dest_ref, wts_ref, h_ref, ys_ref, ws1_ref, ws3_ref, ws2_ref, g_ref, b_ref,
                    o_ref, gbuf, sem, *, tm):
    n = tm * TOP_K

    def row_copy(t, k):
        i = t * TOP_K + k
        return pltpu.make_async_copy(ys_ref.at[pl.ds(dest_ref[0, 0, i], 1)], gbuf.at[pl.ds(i, 1)], sem)

    def start(t, c):
        for k in range(TOP_K):
            row_copy(t, k).start()
        return c

    def wait(t, c):
        for k in range(TOP_K):
            row_copy(t, k).wait()
        return c

    lax.fori_loop(0, tm, start, 0)
    x = h_ref[...]
    xb = x.astype(BF16)
    h1 = _dg(xb, ws1_ref[...], _NN)
    h3 = _dg(xb, ws3_ref[...], _NN)
    acc = _dg((h1 * _sigmoid(h1) * h3).astype(BF16), ws2_ref[...], _NN)

    expand = ((lax.broadcasted_iota(jnp.int32, (TOP_K, n), 1) & (TOP_K - 1))
              == lax.broadcasted_iota(jnp.int32, (TOP_K, n), 0)).astype(BF16)
    own = (lax.shift_right_logical(lax.broadcasted_iota(jnp.int32, (tm, n), 1), TOP_K.bit_length() - 1)
           == lax.broadcasted_iota(jnp.int32, (tm, n), 0))
    w_hi, w_lo = _split2(wts_ref[...])
    sel_hi = jnp.where(own, _dg(w_hi, expand, _NN), 0.0).astype(BF16)
    sel_lo = jnp.where(own, _dg(w_lo, expand, _NN), 0.0).astype(BF16)

    lax.fori_loop(0, tm, wait, 0)
    y_lo, y_hi = _unpack_rows(gbuf[...])
    routed = jnp.concatenate([_dg(sel_hi, y_lo, _NN) + _dg(sel_lo, y_lo, _NN),
                              _dg(sel_hi, y_hi, _NN) + _dg(sel_lo, y_hi, _NN)], axis=-1)
    o_ref[...] = _layer_norm(ALPHA * x + (acc + routed), g_ref[...], b_ref[...])


def _combine(h, ys, dest, wts, ws1, ws3, ws2, g, b, tm):
    t = h.shape[0]
    dest3 = dest.reshape(t // tm, 1, tm * TOP_K)
    row = pl.BlockSpec((1, D_MODEL), lambda i: (0, 0))
    full = lambda a: pl.BlockSpec(a.shape, lambda i: (0, 0))
    return pl.pallas_call(
        functools.partial(_combine_kernel, tm=tm),
        grid=(t // tm,),
        in_specs=[pl.BlockSpec((1, 1, tm * TOP_K), lambda i: (i, 0, 0), memory_space=pltpu.SMEM),
                  pl.BlockSpec((tm, TOP_K), lambda i: (i, 0)),
                  pl.BlockSpec((tm, D_MODEL), lambda i: (i, 0)),
                  pl.BlockSpec(memory_space=pl.ANY),
                  full(ws1), full(ws3), full(ws2), row, row],
        out_specs=pl.BlockSpec((tm, D_MODEL), lambda i: (i, 0)),
        out_shape=jax.ShapeDtypeStruct((t, D_MODEL), F32),
        scratch_shapes=[pltpu.VMEM((tm * TOP_K, HALF), jnp.uint32), pltpu.SemaphoreType.DMA(())],
        compiler_params=_params(("arbitrary",)),
        name="moe_combine",
    )(dest3, wts, h, ys, ws1, ws3, ws2, g, b)


def _moe(h, h_packed, router_w, router_bias, w1, w3, w2, layer, ws1, ws3, ws2, ln_g, ln_b):
    t = h.shape[0]
    eidx, pos, wts, counts = _router(h, router_w.T, router_bias.reshape(N_EXPERTS, 1), tm=min(t, 512))
    counts = counts.reshape(N_EXPERTS)
    pcounts = (counts + EXPERT_ROWS - 1) // EXPERT_ROWS * EXPERT_ROWS
    pends = jnp.cumsum(pcounts)
    pstarts = pends - pcounts
    expert_start = jnp.sum(jnp.where(eidx[None] == jnp.arange(N_EXPERTS)[:, None, None],
                                     pstarts[:, None, None], 0), axis=0)
    dest = (expert_start + pos).T.reshape(-1)
    nb = t * TOP_K // EXPERT_ROWS + N_EXPERTS
    blk_start = jnp.arange(nb, dtype=jnp.int32) * EXPERT_ROWS
    blk_e = jnp.minimum(jnp.sum(blk_start[:, None] >= pends[None, :], axis=1), N_EXPERTS - 1).astype(jnp.int32)
    n_used = (pends[-1] // EXPERT_ROWS).astype(jnp.int32).reshape(1)
    xs = _dispatch(h_packed, dest, jnp.zeros((nb * EXPERT_ROWS, HALF), jnp.uint32), tm=min(t, 256))
    ys = _experts(xs, blk_e, n_used, w1, w3, w2, layer)
    return _combine(h, ys, dest, wts.T, ws1.astype(BF16), ws3.astype(BF16), ws2.astype(BF16),
                    ln_g.reshape(1, -1), ln_b.reshape(1, -1), tm=min(t, 128))


def _diffattn_kernel(qi_ref, ki_ref, q_ref, k_ref, v_ref, lam_ref, sg_ref, o_ref, qs_ref, m_ref, l_ref, acc_ref,
                     *, tq, lam_init):
    step = pl.program_id(1)
    qi = qi_ref[step]
    ki = ki_ref[step]

    @pl.when(ki == 0)
    def _():
        q = q_ref[...]
        map0 = lax.broadcasted_iota(jnp.int32, q.shape, 1) < DIFF_HEAD_DIM
        zero = jnp.zeros_like(q)
        qs_ref[:tq, :] = jnp.where(map0, q, zero)
        qs_ref[tq:, :] = jnp.where(map0, zero, q)
        m_ref[...] = jnp.full_like(m_ref, -jnp.inf)
        l_ref[...] = jnp.zeros_like(l_ref)
        acc_ref[...] = jnp.zeros_like(acc_ref)

    def accumulate(masked):
        s = _dg(qs_ref[...], k_ref[...], _NT)
        if masked:
            row = lax.broadcasted_iota(jnp.int32, s.shape, 0)
            col = lax.broadcasted_iota(jnp.int32, s.shape, 1)
            s = jnp.where(col <= jnp.where(row >= tq, row - tq, row), s, -jnp.inf)
        chunks = [s[:, c * LANES:(c + 1) * LANES] for c in range(tq // LANES)]
        cmax = chunks[0]
        for x in chunks[1:]:
            cmax = jnp.maximum(cmax, x)
        m_old = m_ref[...]
        m_new = jnp.maximum(m_old, jnp.max(cmax, axis=-1, keepdims=True))
        alpha = jnp.exp2(m_old - m_new)
        ps = [jnp.exp2(x - m_new) for x in chunks]
        psum = ps[0]
        for x in ps[1:]:
            psum = psum + x
        l_ref[...] = alpha * l_ref[...] + psum
        p = jnp.concatenate([x.astype(BF16) for x in ps], axis=-1)
        acc_ref[...] = alpha * acc_ref[...] + _dg(p, v_ref[...], _NN)
        m_ref[...] = m_new

    @pl.when(ki < qi)
    def _():
        accumulate(False)

    @pl.when(ki == qi)
    def _():
        accumulate(True)
        lp = lam_ref[...]
        lam = (jnp.exp(jnp.sum(lp[0:1] * lp[1:2], axis=-1, keepdims=True))
               - jnp.exp(jnp.sum(lp[2:3] * lp[3:4], axis=-1, keepdims=True)) + lam_init)
        l = jnp.sum(l_ref[...], axis=-1, keepdims=True)
        o = acc_ref[:tq, :] / l[:tq, :] - lam * (acc_ref[tq:, :] / l[tq:, :])
        o = o * lax.rsqrt(jnp.mean(o * o, axis=-1, keepdims=True) + RMS_EPS)
        o_ref[...] = o * sg_ref[...] * (1.0 - lam_init)


def _diffattn(q, kv, lam_params, subln_g, lam_init, tq):
    t = q.shape[0]
    nh = DIFF_HEADS
    nq = t // tq
    pairs = [(a, b) for a in range(nq) for b in range(a + 1)]
    qi_tab = jnp.asarray([a for a, _ in pairs], jnp.int32)
    ki_tab = jnp.asarray([b for _, b in pairs], jnp.int32)
    grid_spec = pltpu.PrefetchScalarGridSpec(
        num_scalar_prefetch=2,
        grid=(nh, len(pairs)),
        in_specs=[pl.BlockSpec((tq, LANES), lambda h, s, qt, kt: (qt[s], h)),
                  pl.BlockSpec((tq, LANES), lambda h, s, qt, kt: (kt[s], h)),
                  pl.BlockSpec((tq, LANES), lambda h, s, qt, kt: (kt[s], nh + h)),
                  pl.BlockSpec(lam_params.shape, lambda h, s, qt, kt: (0, 0)),
                  pl.BlockSpec((1, LANES), lambda h, s, qt, kt: (0, 0))],
        out_specs=pl.BlockSpec((tq, LANES), lambda h, s, qt, kt: (qt[s], h)),
        scratch_shapes=[pltpu.VMEM((2 * tq, LANES), BF16),
                        pltpu.VMEM((2 * tq, LANES), F32),
                        pltpu.VMEM((2 * tq, LANES), F32),
                        pltpu.VMEM((2 * tq, LANES), F32)],
    )
    return pl.pallas_call(
        functools.partial(_diffattn_kernel, tq=tq, lam_init=lam_init),
        grid_spec=grid_spec,
        out_shape=jax.ShapeDtypeStruct((t, MAIN_WIDTH), F32),
        compiler_params=_params(("parallel", "arbitrary")),
        name="diff_attn",
    )(qi_tab, ki_tab, q, kv, kv, lam_params, subln_g)


def _pad_rows(w, rows):
    return jnp.pad(w, ((0, rows - w.shape[0]), (0, 0)))


def _layer_a(h, memkv, a_w_in, a_mu, a_w0, a_w_decay_up, a_a0, a_w_a_up, a_w_g_up, a_k_k, a_k_a,
             a_r_k, a_gn_g, a_gn_b, w_out, ln1_g, ln1_b, tm):
    c = MAIN_WIDTH
    lo = 3 * c

    def pad_cols(m):
        pieces = [m[..., :lo],
                  jnp.pad(m[..., lo:lo + DECAY_LORA], ((0, 0), (0, LANES - DECAY_LORA))),
                  jnp.pad(m[..., lo + DECAY_LORA:lo + DECAY_LORA + AAA_LORA], ((0, 0), (0, LANES - AAA_LORA))),
                  jnp.pad(m[..., lo + DECAY_LORA + AAA_LORA:lo + DECAY_LORA + AAA_LORA + GATE_LORA],
                          ((0, 0), (0, 2 * LANES - GATE_LORA))),
                  m[..., lo + DECAY_LORA + AAA_LORA + GATE_LORA:]]
        return jnp.concatenate(pieces, axis=-1)

    w_in = pad_cols(a_w_in).astype(BF16)
    mu_pad = pad_cols(jnp.concatenate([a_mu, jnp.zeros((MEM_WIDTH,), F32)]).reshape(1, -1))[:, :lo + LORA_TILE]
    z = _proj(h, w_in, tm=tm, tn=512, out_dtype=F32)
    lane = jnp.arange(LANES)
    g128 = (lane[:, None] // RWKV_HEAD_DIM == lane[None, :] // RWKV_HEAD_DIM).astype(BF16)
    row = lambda a: a.reshape(1, -1)
    r, lw, k, v, an, b, g = _rwkv_prep(
        z, mu_pad, row(a_w0), _pad_rows(a_w_decay_up, LANES), row(a_a0), _pad_rows(a_w_a_up, LANES),
        _pad_rows(a_w_g_up, 2 * LANES), row(a_k_k), row(a_k_a), g128, tm=min(tm, 256))
    main = _wkv(r, lw, k, v, an, b, g, row(a_r_k), row(a_gn_g), row(a_gn_b), g128)
    memo = _memattn(z, (lo + LORA_TILE) // MEM_WIDTH, memkv, tm=min(tm, 512))
    return _outproj_ln(main, memo, h, w_out.astype(BF16), row(ln1_g), row(ln1_b), tm=min(tm, 256))


def _layer_b(h, memkv, positions, kv_shared_w, b_w_in, lam_params, subln_g, lam_init, w_out, ln1_g, ln1_b, tm):
    c = MAIN_WIDTH
    rope_k = _rope_tables(positions, tm=min(tm, 512), scale=1.0)
    rope_q = _rope_tables(positions, tm=min(tm, 512), scale=DIFF_HEAD_DIM ** -0.5 * math.log2(math.e))
    kv = _proj(h, kv_shared_w.astype(BF16), tm=tm, tn=512, out_dtype=BF16, rope=rope_k, n_rope=c // 512)
    zq = _proj(h, b_w_in.astype(BF16), tm=tm, tn=512, out_dtype=BF16, rope=rope_q, n_rope=c // 512)
    main = _diffattn(zq, kv, lam_params, subln_g.reshape(1, -1), lam_init, tq=min(tm, ATTN_TILE))
    memo = _memattn(zq, c // MEM_WIDTH, memkv, tm=min(tm, 512))
    row = lambda a: a.reshape(1, -1)
    return _outproj_ln(main, memo, h, w_out.astype(BF16), row(ln1_g), row(ln1_b), tm=min(tm, 256))


def kernel(x, mem, positions, a_w_in, a_mu, a_w0, a_w_decay_up, a_a0, a_w_a_up, a_w_g_up, a_k_k, a_k_a, a_r_k, a_gn_g, a_gn_b, kv_shared_w, b_w_in, b_lambda_q1, b_lambda_k1, b_lambda_q2, b_lambda_k2, b_subln_g, mem_w_kv, w_out, ln1_g, ln1_b, ln2_g, ln2_b, router_w, router_bias, exp_w1, exp_w3, exp_w2, sh_w1, sh_w3, sh_w2):
    bsz, t, _ = x.shape
    assert bsz == 1
    h = x.reshape(t, D_MODEL)
    tm = min(t, 1024)
    mem2 = mem.reshape(-1, D_MODEL)
    n_a = DEPTH // 2
    for l in range(DEPTH):
        memkv = _proj(mem2, mem_w_kv[l].astype(BF16), tm=mem2.shape[0], tn=512, out_dtype=F32)
        if l < n_a:
            h, h_packed = _layer_a(h, memkv, a_w_in[l], a_mu[l], a_w0[l], a_w_decay_up[l], a_a0[l], a_w_a_up[l],
                         a_w_g_up[l], a_k_k[l], a_k_a[l], a_r_k[l], a_gn_g[l], a_gn_b[l],
                         w_out[l], ln1_g[l], ln1_b[l], tm)
        else:
            j = l - n_a
            lam_init = 0.8 - 0.6 * math.exp(-0.3 * l)
            lam_params = jnp.stack([b_lambda_q1[j], b_lambda_k1[j], b_lambda_q2[j], b_lambda_k2[j]])
            h, h_packed = _layer_b(h, memkv, positions, kv_shared_w, b_w_in[j], lam_params, b_subln_g[j], lam_init,
                         w_out[l], ln1_g[l], ln1_b[l], tm)
        h = _moe(h, h_packed, router_w[l], router_bias[l], exp_w1, exp_w3, exp_w2, l,
                 sh_w1[l], sh_w3[l], sh_w2[l], ln2_g[l], ln2_b[l])
    return h.reshape(bsz, t, D_MODEL)
```

```python
import functools
import math

import jax
import jax.numpy as jnp
from jax import lax
from jax.experimental import pallas as pl
from jax.experimental.pallas import tpu as pltpu

F32 = jnp.float32
BF16 = jnp.bfloat16

D_MODEL = 2048
DEPTH = 2
MEM_HEADS = 4
MEM_WIDTH = D_MODEL // 4
MEM_HEAD_DIM = MEM_WIDTH // MEM_HEADS
MAIN_WIDTH = D_MODEL - MEM_WIDTH
RWKV_HEAD_DIM = 64
DECAY_LORA = 64
AAA_LORA = 64
GATE_LORA = 224
GN_EPS = 64e-5
DIFF_HEADS = 12
DIFF_HEAD_DIM = 64
ROPE_DIM = DIFF_HEAD_DIM // 4
ROPE_THETA = 500000.0
N_EXPERTS = 64
TOP_K = 8
N_GROUPS = 8
GROUP_SIZE = N_EXPERTS // N_GROUPS
TOPK_GROUPS = 4
EXPERT_DIM = D_MODEL // 4
ROUTED_SCALE = 2.5
ALPHA = (2 * DEPTH) ** 0.25
LN_EPS = 1e-5
RMS_EPS = 1e-5

LANES = 128
VMEM_LIMIT = 56 * 1024 * 1024

LORA_TILE = 512
A_IN_PAD = 3 * MAIN_WIDTH + LORA_TILE + MEM_WIDTH
WKV_CHUNK = 64
WKV_PAIRS = 12
EXPERT_ROWS = 256
ATTN_TILE = 512


def _params(sem):
    return pltpu.CompilerParams(dimension_semantics=sem, vmem_limit_bytes=VMEM_LIMIT)


def _split2(x):
    hi = x.astype(BF16)
    lo = (x - hi.astype(F32)).astype(BF16)
    return hi, lo


def _dg(a, b, dims):
    return lax.dot_general(a, b, (dims, ((), ())), preferred_element_type=F32)


_NN = ((1,), (0,))
_NT = ((1,), (1,))
_TN = ((0,), (0,))


def _dot3(a, b, dims=_NN):
    ah, al = _split2(a)
    bh, bl = _split2(b)
    return _dg(ah, bh, dims) + (_dg(ah, bl, dims) + _dg(al, bh, dims))


def _dot2l(a, b_exact, dims=_NN):
    ah, al = _split2(a)
    return _dg(ah, b_exact, dims) + _dg(al, b_exact, dims)


def _dot1(a, b, dims=_NN):
    return _dg(a.astype(BF16), b.astype(BF16), dims)


def _sigmoid(x):
    return 1.0 / (1.0 + jnp.exp(-x))


def _proj_kernel(*refs, n_rope, tn, cast_x):
    if n_rope:
        x_ref, w_ref, c_ref, sa_ref, sb_ref, o_ref = refs[:6]
        rest = refs[6:]
    else:
        x_ref, w_ref, o_ref = refs[:3]
        rest = refs[3:]
    j = pl.program_id(1)
    if cast_x:
        xb_ref, = rest

        @pl.when(j == 0)
        def _():
            xb_ref[...] = x_ref[...].astype(BF16)
        xb = xb_ref[...]
    else:
        xb = x_ref[...]
    acc = jnp.dot(xb, w_ref[...], preferred_element_type=F32)
    if n_rope:
        @pl.when(j < n_rope)
        def _():
            c, sa, sb = c_ref[...], sa_ref[...], sb_ref[...]
            for q in range(tn // LANES):
                a = acc[:, q * LANES:(q + 1) * LANES]
                rot = a * c + pltpu.roll(a, LANES - ROPE_DIM // 2, 1) * sa + pltpu.roll(a, ROPE_DIM // 2, 1) * sb
                o_ref[:, q * LANES:(q + 1) * LANES] = rot.astype(o_ref.dtype)

        @pl.when(j >= n_rope)
        def _():
            o_ref[...] = acc.astype(o_ref.dtype)
    else:
        o_ref[...] = acc.astype(o_ref.dtype)


def _proj(x, w, *, tm, tn, out_dtype, rope=None, n_rope=0):
    m, k = x.shape
    n = w.shape[1]
    assert m % tm == 0 and n % tn == 0
    cast_x = x.dtype != BF16
    in_specs = [pl.BlockSpec((tm, k), lambda i, j: (i, 0)),
                pl.BlockSpec((k, tn), lambda i, j: (0, j))]
    args = [x, w]
    if n_rope:
        in_specs += [pl.BlockSpec((tm, LANES), lambda i, j: (i, 0))] * 3
        args += list(rope)
    return pl.pallas_call(
        functools.partial(_proj_kernel, n_rope=n_rope, tn=tn, cast_x=cast_x),
        grid=(m // tm, n // tn),
        in_specs=in_specs,
        out_specs=pl.BlockSpec((tm, tn), lambda i, j: (i, j)),
        out_shape=jax.ShapeDtypeStruct((m, n), out_dtype),
        scratch_shapes=[pltpu.VMEM((tm, k), BF16)] if cast_x else [],
        compiler_params=_params(("parallel", "arbitrary")),
        name="proj",
    )(*args)


def _rope_table_kernel(pos_ref, invf_ref, c_ref, sa_ref, sb_ref, *, scale):
    ang = pos_ref[...].astype(F32) * invf_ref[...]
    lane = lax.broadcasted_iota(jnp.int32, ang.shape, 1) & (DIFF_HEAD_DIM - 1)
    c = jnp.cos(ang)
    s = jnp.sin(ang)
    half = ROPE_DIM // 2
    c_ref[...] = jnp.where(lane < ROPE_DIM, c, 1.0) * scale
    sa_ref[...] = jnp.where(lane < half, -s, 0.0) * scale
    sb_ref[...] = jnp.where(lane < half, 0.0, jnp.where(lane < ROPE_DIM, s, 0.0)) * scale


def _rope_tables(positions, tm, scale):
    t = positions.shape[1]
    half = ROPE_DIM // 2
    inv_freq = ROPE_THETA ** (-jnp.arange(half, dtype=F32) / half)
    lane = jnp.arange(LANES) % DIFF_HEAD_DIM
    invf = jnp.where(lane < ROPE_DIM, inv_freq[lane % half], 0.0).reshape(1, LANES).astype(F32)
    pos = positions.reshape(t, 1)
    shp = jax.ShapeDtypeStruct((t, LANES), F32)
    return pl.pallas_call(
        functools.partial(_rope_table_kernel, scale=scale),
        grid=(t // tm,),
        in_specs=[pl.BlockSpec((tm, 1), lambda i: (i, 0)), pl.BlockSpec((1, LANES), lambda i: (0, 0))],
        out_specs=[pl.BlockSpec((tm, LANES), lambda i: (i, 0))] * 3,
        out_shape=[shp, shp, shp],
        compiler_params=_params(("parallel",)),
        name="rope_tables",
    )(pos, invf)


def _head_sum(x, g128):
    return _dot2l(x, g128)


def _prep_kernel(z_ref, mu_ref, w0_ref, wdu_ref, a0_ref, wau_ref, wgu_ref, kk_ref, ka_ref, g128_ref,
                 r_o, lw_o, k_o, v_o, an_o, b_o, g_o, carry_ref, *, tm):
    i = pl.program_id(0)

    @pl.when(i == 0)
    def _():
        carry_ref[...] = jnp.zeros_like(carry_ref)

    row0 = lax.broadcasted_iota(jnp.int32, (tm, 1), 0) == 0
    c = MAIN_WIDTH

    def shifted(c0, c1):
        z = z_ref[:, c0:c1]
        prev = jnp.where(row0, carry_ref[:, c0:c1], pltpu.roll(z, 1, 0))
        return z + (prev - z) * mu_ref[:, c0:c1]

    r = shifted(0, c)
    k = shifted(c, 2 * c)
    v = shifted(2 * c, 3 * c)
    lora = shifted(3 * c, 3 * c + LORA_TILE)
    carry_ref[...] = z_ref[pl.ds(tm - 1, 1), :]

    dw = lora[:, 0:LANES]
    da = lora[:, LANES:2 * LANES]
    dg = lora[:, 2 * LANES:4 * LANES]
    wl = w0_ref[...] + _dot3(jnp.tanh(dw), wdu_ref[...])
    lw = -_sigmoid(wl) * math.exp(-0.5)
    a = _sigmoid(a0_ref[...] + _dot3(da, wau_ref[...]))
    g = _dot3(_sigmoid(dg), wgu_ref[...])
    kk = k * kk_ref[...]
    g128 = g128_ref[...]
    for q in range(c // LANES):
        sl = slice(q * LANES, (q + 1) * LANES)
        kq = kk[:, sl]
        nrm = jnp.maximum(jnp.sqrt(_head_sum(kq * kq, g128)), 1e-12)
        kn = kq / nrm
        an_o[:, sl] = -kn
        b_o[:, sl] = kn * a[:, sl]
    r_o[...] = r
    lw_o[...] = lw
    k_o[...] = k * (1.0 + (a - 1.0) * ka_ref[...])
    v_o[...] = v
    g_o[...] = g


def _rwkv_prep(z, mu_pad, w0, wdu_pad, a0, wau_pad, wgu_pad, k_k, k_a, g128, tm):
    t = z.shape[0]
    c = MAIN_WIDTH
    wz = 3 * c + LORA_TILE
    row = lambda n: pl.BlockSpec((1, n), lambda i: (0, 0))
    full = lambda a: pl.BlockSpec(a.shape, lambda i: (0, 0))
    shp = jax.ShapeDtypeStruct((t, c), F32)
    return pl.pallas_call(
        functools.partial(_prep_kernel, tm=tm),
        grid=(t // tm,),
        in_specs=[pl.BlockSpec((tm, wz), lambda i: (i, 0)), row(wz), row(c), full(wdu_pad), row(c),
                  full(wau_pad), full(wgu_pad), row(c), row(c), full(g128)],
        out_specs=[pl.BlockSpec((tm, c), lambda i: (i, 0))] * 7,
        out_shape=[shp] * 7,
        scratch_shapes=[pltpu.VMEM((1, wz), F32)],
        compiler_params=_params(("arbitrary",)),
        name="rwkv_prep",
    )(z, mu_pad, w0, wdu_pad, a0, wau_pad, wgu_pad, k_k, k_a, g128)


def _wkv_kernel(r_ref, lw_ref, k_ref, v_ref, an_ref, b_ref, g_ref, rk_ref, gng_ref, gnb_ref, g128_ref,
                o_ref, ht_ref, *, L, pairs):
    ci = pl.program_id(1)

    @pl.when(ci == 0)
    def _():
        ht_ref[...] = jnp.zeros_like(ht_ref)

    P = range(pairs)
    cols = [slice(p * LANES, (p + 1) * LANES) for p in P]
    ld = lambda ref: [ref[:, c] for c in cols]
    r, lw, k, v, an, b = ld(r_ref), ld(lw_ref), ld(k_ref), ld(v_ref), ld(an_ref), ld(b_ref)
    g128 = g128_ref[...]
    n2 = 2 * L

    rowi = lax.broadcasted_iota(jnp.int32, (L, L), 0)
    coli = lax.broadcasted_iota(jnp.int32, (L, L), 1)
    tri = (coli <= rowi).astype(BF16)

    def running_sum(x):
        x_h = x.astype(BF16)
        x_r = x - x_h.astype(F32)
        x_m = x_r.astype(BF16)
        x_l = (x_r - x_m.astype(F32)).astype(BF16)
        return _dg(tri, x_h, _NN) + (_dg(tri, x_m, _NN) + _dg(tri, x_l, _NN))

    cum = [running_sum(lw[p]) for p in P]
    cum_last = [cum[p][L - 1:L, :] for p in P]
    p_inv = [jnp.exp(-cum[p]) for p in P]
    at = [an[p] * jnp.exp(cum[p] - lw[p]) for p in P]
    rt = [r[p] * jnp.exp(cum[p]) for p in P]
    bt = [b[p] * p_inv[p] for p in P]
    kt = [k[p] * p_inv[p] for p in P]
    p_end = [jnp.exp(cum_last[p] - cum[p]) for p in P]

    head0 = lax.broadcasted_iota(jnp.int32, (L, LANES), 1) < RWKV_HEAD_DIM

    def stack(x):
        return jnp.concatenate([jnp.where(head0, x, 0.0), jnp.where(head0, 0.0, x)], axis=0)

    ar = [jnp.concatenate([stack(at[p]), stack(rt[p])], axis=0) for p in P]
    bk = [jnp.concatenate([stack(bt[p]), stack(kt[p])], axis=0) for p in P]
    v_st = [stack(v[p]) for p in P]
    m = [_dot3(ar[p], bk[p], _NT) for p in P]

    ri = lax.broadcasted_iota(jnp.int32, (n2, n2), 0)
    cj = lax.broadcasted_iota(jnp.int32, (n2, n2), 1)
    same = (ri >= L) == (cj >= L)
    strict = same & (cj < ri)
    incl = same & (cj <= ri)
    ab = [jnp.where(strict, m[p][:n2, :n2], 0.0) for p in P]
    akrk = [jnp.concatenate([jnp.where(strict, m[p][:n2, n2:], 0.0),
                             jnp.where(incl, m[p][n2:, n2:], 0.0)], axis=0) for p in P]
    rb = [jnp.where(incl, m[p][n2:, :n2], 0.0) for p in P]

    ht = [ht_ref[p] for p in P]
    arh = [_dot3(ar[p], ht[p], _NT) for p in P]
    kv = [_dot3(akrk[p], v_st[p]) for p in P]
    u = [arh[p][:n2] + kv[p][:n2] for p in P]
    apow = ab
    steps = max(1, (L - 1).bit_length())
    for s in range(steps):
        u = [u[p] + _dot1(apow[p], u[p]) for p in P]
        if s + 1 < steps:
            apow = [_dot1(apow[p], apow[p]) for p in P]
    y_st = [arh[p][n2:] + kv[p][n2:] + _dot1(rb[p], u[p]) for p in P]
    y = [y_st[p][:L] + y_st[p][L:] for p in P]

    bk_end = [jnp.concatenate([stack(b[p] * p_end[p]), stack(k[p] * p_end[p])], axis=0) for p in P]
    for p in P:
        uv = jnp.concatenate([u[p], v_st[p]], axis=0)
        ht_ref[p] = ht[p] * jnp.exp(cum_last[p]) + _dot3(uv, bk_end[p], _TN)

    inv_n = 1.0 / RWKV_HEAD_DIM
    mu = [_head_sum(y[p], g128) * inv_n for p in P]
    yc = [y[p] - mu[p] for p in P]
    var = [_head_sum(yc[p] * yc[p], g128) * inv_n for p in P]
    bonus = [_head_sum(r[p] * k[p] * rk_ref[:, cols[p]], g128) * v[p] for p in P]
    for p in P:
        yn = yc[p] * lax.rsqrt(var[p] + GN_EPS) * gng_ref[:, cols[p]] + gnb_ref[:, cols[p]]
        o_ref[:, cols[p]] = (yn + bonus[p]) * g_ref[:, cols[p]]


def _wkv(r, lw, k, v, an, b, g, r_k, gn_g, gn_b, g128):
    t, c = r.shape
    L = WKV_CHUNK
    w = WKV_PAIRS * LANES
    tok = pl.BlockSpec((L, w), lambda p, ci: (ci, p))
    par = pl.BlockSpec((1, w), lambda p, ci: (0, p))
    return pl.pallas_call(
        functools.partial(_wkv_kernel, L=L, pairs=WKV_PAIRS),
        grid=(c // w, t // L),
        in_specs=[tok] * 7 + [par] * 3 + [pl.BlockSpec((LANES, LANES), lambda p, ci: (0, 0))],
        out_specs=tok,
        out_shape=jax.ShapeDtypeStruct((t, c), F32),
        scratch_shapes=[pltpu.VMEM((WKV_PAIRS, LANES, LANES), F32)],
        compiler_params=_params(("parallel", "arbitrary")),
        name="wkv7",
    )(r, lw, k, v, an, b, g, r_k, gn_g, gn_b, g128)


def _memattn_kernel(q_ref, kv_ref, o_ref):
    scale = MEM_HEAD_DIM ** -0.5
    for hh in range(MEM_HEADS):
        sl = slice(hh * MEM_HEAD_DIM, (hh + 1) * MEM_HEAD_DIM)
        q = q_ref[:, sl].astype(BF16)
        kh = kv_ref[:, sl].astype(BF16)
        vh = kv_ref[:, MEM_WIDTH + hh * MEM_HEAD_DIM:MEM_WIDTH + (hh + 1) * MEM_HEAD_DIM].astype(BF16)
        s = _dg(q, kh, _NT) * scale
        s = s - jnp.max(s, axis=-1, keepdims=True)
        e = jnp.exp(s)
        p = e / jnp.sum(e, axis=-1, keepdims=True)
        o_ref[:, sl] = _dg(p.astype(BF16), vh, _NN)


def _memattn(zq, q_col_block, memkv, tm):
    t = zq.shape[0]
    return pl.pallas_call(
        _memattn_kernel,
        grid=(t // tm,),
        in_specs=[pl.BlockSpec((tm, MEM_WIDTH), lambda i: (i, q_col_block)),
                  pl.BlockSpec(memkv.shape, lambda i: (0, 0))],
        out_specs=pl.BlockSpec((tm, MEM_WIDTH), lambda i: (i, 0)),
        out_shape=jax.ShapeDtypeStruct((t, MEM_WIDTH), F32),
        compiler_params=_params(("parallel",)),
        name="mem_attn",
    )(zq, memkv)


def _layer_norm(x, g, b):
    mu = jnp.mean(x, axis=-1, keepdims=True)
    xc = x - mu
    var = jnp.mean(xc * xc, axis=-1, keepdims=True)
    return xc * lax.rsqrt(var + LN_EPS) * g + b


HALF = D_MODEL // 2


def _pack_rows(x):
    lo = pltpu.bitcast(x[:, :HALF].astype(BF16).astype(F32), jnp.uint32)
    hi = pltpu.bitcast(x[:, HALF:].astype(BF16).astype(F32), jnp.uint32)
    return hi | lax.shift_right_logical(lo, jnp.uint32(16))


def _unpack_rows(p):
    lo = pltpu.bitcast(lax.shift_left(p, jnp.uint32(16)), F32).astype(BF16)
    hi = pltpu.bitcast(p & jnp.uint32(0xFFFF0000), F32).astype(BF16)
    return lo, hi


def _outproj_kernel(main_ref, memo_ref, h_ref, w_ref, g_ref, b_ref, o_ref, op_ref):
    mix = (_dg(main_ref[...].astype(BF16), w_ref[:MAIN_WIDTH, :], _NN)
           + _dg(memo_ref[...].astype(BF16), w_ref[MAIN_WIDTH:, :], _NN))
    y = _layer_norm(ALPHA * h_ref[...] + mix, g_ref[...], b_ref[...])
    o_ref[...] = y
    op_ref[...] = _pack_rows(y)


def _outproj_ln(main, memo, h, w_bf, g, b, tm):
    t = h.shape[0]
    row = pl.BlockSpec((1, D_MODEL), lambda i: (0, 0))
    return pl.pallas_call(
        _outproj_kernel,
        grid=(t // tm,),
        in_specs=[pl.BlockSpec((tm, MAIN_WIDTH), lambda i: (i, 0)),
                  pl.BlockSpec((tm, MEM_WIDTH), lambda i: (i, 0)),
                  pl.BlockSpec((tm, D_MODEL), lambda i: (i, 0)),
                  pl.BlockSpec((D_MODEL, D_MODEL), lambda i: (0, 0)), row, row],
        out_specs=[pl.BlockSpec((tm, D_MODEL), lambda i: (i, 0)),
                   pl.BlockSpec((tm, HALF), lambda i: (i, 0))],
        out_shape=[jax.ShapeDtypeStruct((t, D_MODEL), F32),
                   jax.ShapeDtypeStruct((t, HALF), jnp.uint32)],
        compiler_params=_params(("parallel",)),
        name="outproj_ln",
    )(main, memo, h, w_bf, g, b)


def _first_max(vals, ids):
    m = vals[0]
    for x in vals[1:]:
        m = jnp.maximum(m, x)
    m = jnp.max(m, axis=0, keepdims=True)
    big = jnp.int32(1 << 20)
    idx = None
    for x, e in zip(vals, ids):
        cand = jnp.where(x == m, e, big)
        idx = cand if idx is None else jnp.minimum(idx, cand)
    idx = jnp.min(idx, axis=0, keepdims=True)
    return m, idx


def _router_kernel(h_ref, wt_ref, bias_ref, eidx_ref, pos_ref, wts_ref, cnt_ref, run_ref, *, tm):
    i = pl.program_id(0)

    @pl.when(i == 0)
    def _():
        run_ref[...] = jnp.zeros_like(run_ref)

    neg = -jnp.inf
    logits = _dot3(wt_ref[...], h_ref[...], _NT)
    scores = _sigmoid(logits)
    biased = scores + bias_ref[...]
    sub = lax.broadcasted_iota(jnp.int32, (GROUP_SIZE, tm), 0)
    grp = [biased[g * GROUP_SIZE:(g + 1) * GROUP_SIZE, :] for g in range(N_GROUPS)]
    sc = [scores[g * GROUP_SIZE:(g + 1) * GROUP_SIZE, :] for g in range(N_GROUPS)]
    ids = [sub + g * GROUP_SIZE for g in range(N_GROUPS)]

    gs = []
    for g in range(N_GROUPS):
        m1, i1 = _first_max([grp[g]], [sub])
        m2 = jnp.max(jnp.where(sub == i1, neg, grp[g]), axis=0, keepdims=True)
        gs.append(m1 + m2)
    gscore = jnp.concatenate(gs, axis=0)
    gmask = jnp.zeros((N_GROUPS, tm), jnp.bool_)
    cur = gscore
    for _ in range(TOPK_GROUPS):
        _, gi = _first_max([cur], [sub])
        hit = sub == gi
        gmask = gmask | hit
        cur = jnp.where(hit, neg, cur)
    gm = gmask.astype(F32)
    cur = [jnp.where(gm[g:g + 1, :] > 0.0, grp[g], neg) for g in range(N_GROUPS)]

    sel = [jnp.zeros((GROUP_SIZE, tm), jnp.bool_) for _ in range(N_GROUPS)]
    e_rows, w_rows = [], []
    for _ in range(TOP_K):
        _, ei = _first_max(cur, ids)
        w = jnp.zeros((1, tm), F32)
        for g in range(N_GROUPS):
            hit = ids[g] == ei
            sel[g] = sel[g] | hit
            cur[g] = jnp.where(hit, neg, cur[g])
            w = w + jnp.sum(jnp.where(hit, sc[g], 0.0), axis=0, keepdims=True)
        e_rows.append(ei)
        w_rows.append(w)
    eidx = jnp.concatenate(e_rows, axis=0)
    wraw = jnp.concatenate(w_rows, axis=0)
    wts_ref[...] = wraw / (jnp.sum(wraw, axis=0, keepdims=True) + 1e-20) * ROUTED_SCALE
    eidx_ref[...] = eidx

    self32 = jnp.concatenate([s.astype(F32) for s in sel], axis=0)
    ui = lax.broadcasted_iota(jnp.int32, (tm, tm), 0)
    uj = lax.broadcasted_iota(jnp.int32, (tm, tm), 1)
    before = (ui < uj).astype(BF16)
    local = _dg(self32.astype(BF16), before, _NN)
    run = run_ref[...]
    glob = local + run
    run_new = run + jnp.sum(self32, axis=1, keepdims=True)
    run_ref[...] = run_new
    cnt_ref[...] = run_new.astype(jnp.int32)
    e64 = lax.broadcasted_iota(jnp.int32, (N_EXPERTS, tm), 0)
    p_rows = [jnp.sum(jnp.where(e64 == e_rows[k], glob, 0.0), axis=0, keepdims=True) for k in range(TOP_K)]
    pos_ref[...] = jnp.concatenate(p_rows, axis=0).astype(jnp.int32)


def _router(h, w_router_t, bias_col, tm):
    t = h.shape[0]
    return pl.pallas_call(
        functools.partial(_router_kernel, tm=tm),
        grid=(t // tm,),
        in_specs=[pl.BlockSpec((tm, D_MODEL), lambda i: (i, 0)),
                  pl.BlockSpec((N_EXPERTS, D_MODEL), lambda i: (0, 0)),
                  pl.BlockSpec((N_EXPERTS, 1), lambda i: (0, 0))],
        out_specs=[pl.BlockSpec((TOP_K, tm), lambda i: (0, i)),
                   pl.BlockSpec((TOP_K, tm), lambda i: (0, i)),
                   pl.BlockSpec((TOP_K, tm), lambda i: (0, i)),
                   pl.BlockSpec((N_EXPERTS, 1), lambda i: (0, 0))],
        out_shape=[jax.ShapeDtypeStruct((TOP_K, t), jnp.int32),
                   jax.ShapeDtypeStruct((TOP_K, t), jnp.int32),
                   jax.ShapeDtypeStruct((TOP_K, t), F32),
                   jax.ShapeDtypeStruct((N_EXPERTS, 1), jnp.int32)],
        scratch_shapes=[pltpu.VMEM((N_EXPERTS, 1), F32)],
        compiler_params=_params(("arbitrary",)),
        name="router",
    )(h, w_router_t, bias_col)


def _dispatch_kernel(dest_ref, h_ref, xs_in_ref, xs_ref, sem, *, tm):
    del xs_in_ref

    def row_copy(t, k):
        return pltpu.make_async_copy(h_ref.at[pl.ds(t, 1)],
                                     xs_ref.at[pl.ds(dest_ref[0, 0, t * TOP_K + k], 1)], sem)

    def start(t, c):
        for k in range(TOP_K):
            row_copy(t, k).start()
        return c

    def wait(t, c):
        for k in range(TOP_K):
            row_copy(t, k).wait()
        return c

    lax.fori_loop(0, tm, start, 0)
    lax.fori_loop(0, tm, wait, 0)


def _dispatch(h, dest, xs_init, tm):
    t = h.shape[0]
    dest3 = dest.reshape(t // tm, 1, tm * TOP_K)
    return pl.pallas_call(
        functools.partial(_dispatch_kernel, tm=tm),
        grid=(t // tm,),
        in_specs=[pl.BlockSpec((1, 1, tm * TOP_K), lambda i: (i, 0, 0), memory_space=pltpu.SMEM),
                  pl.BlockSpec((tm, h.shape[1]), lambda i: (i, 0)),
                  pl.BlockSpec(memory_space=pl.ANY)],
        out_specs=pl.BlockSpec(memory_space=pl.ANY),
        out_shape=jax.ShapeDtypeStruct(xs_init.shape, xs_init.dtype),
        scratch_shapes=[pltpu.SemaphoreType.DMA(())],
        input_output_aliases={2: 0},
        compiler_params=_params(("arbitrary",)),
        name="moe_dispatch",
    )(dest3, h, xs_init)


def _expert_kernel(blk_run_ref, run_e_ref, meta_ref, x_ref, w1_hbm, w3_hbm, w2_hbm, y_ref,
                   w1f, w3f, w2f, w1b, w3b, w2b, sem, *, layer):
    b = pl.program_id(0)
    n_used, n_runs = meta_ref[0], meta_ref[1]
    used = b < n_used
    run = blk_run_ref[b]
    fresh = jnp.logical_or(b == 0, run != blk_run_ref[jnp.maximum(b - 1, 0)])

    def fetch(r, slot):
        e = run_e_ref[r]
        return (pltpu.make_async_copy(w1_hbm.at[layer, e], w1f.at[slot], sem.at[slot]),
                pltpu.make_async_copy(w3_hbm.at[layer, e], w3f.at[slot], sem.at[slot]),
                pltpu.make_async_copy(w2_hbm.at[layer, e], w2f.at[slot], sem.at[slot]))

    @pl.when(jnp.logical_and(used, fresh))
    def _():
        slot = run & 1

        @pl.when(b == 0)
        def _():
            for c in fetch(run, slot):
                c.start()

        @pl.when(run + 1 < n_runs)
        def _():
            for c in fetch(run + 1, 1 - slot):
                c.start()

        for c in fetch(run, slot):
            c.wait()
        w1b[...] = w1f[slot].astype(BF16)
        w3b[...] = w3f[slot].astype(BF16)
        w2b[...] = w2f[slot].astype(BF16)

    @pl.when(used)
    def _():
        x_lo, x_hi = _unpack_rows(x_ref[...])
        h1 = _dg(x_lo, w1b[:HALF, :], _NN) + _dg(x_hi, w1b[HALF:, :], _NN)
        h3 = _dg(x_lo, w3b[:HALF, :], _NN) + _dg(x_hi, w3b[HALF:, :], _NN)
        hb = (h1 * _sigmoid(h1) * h3).astype(BF16)
        y_ref[...] = _pack_rows(_dg(hb, w2b[...], _NN))

    @pl.when(jnp.logical_not(used))
    def _():
        y_ref[...] = jnp.zeros_like(y_ref)


def _experts(xs, blk_run, run_e, meta, w1, w3, w2, layer):
    cap = xs.shape[0]
    nb = cap // EXPERT_ROWS
    xmap = lambda b, br, re, mt: (jnp.minimum(b, mt[0] - 1), 0)
    hbm = pl.BlockSpec(memory_space=pl.ANY)
    grid_spec = pltpu.PrefetchScalarGridSpec(
        num_scalar_prefetch=3,
        grid=(nb,),
        in_specs=[pl.BlockSpec((EXPERT_ROWS, HALF), xmap), hbm, hbm, hbm],
        out_specs=pl.BlockSpec((EXPERT_ROWS, HALF), lambda b, br, re, mt: (b, 0)),
        scratch_shapes=[pltpu.VMEM((2, D_MODEL, EXPERT_DIM), F32),
                        pltpu.VMEM((2, D_MODEL, EXPERT_DIM), F32),
                        pltpu.VMEM((2, EXPERT_DIM, D_MODEL), F32),
                        pltpu.VMEM((D_MODEL, EXPERT_DIM), BF16),
                        pltpu.VMEM((D_MODEL, EXPERT_DIM), BF16),
                        pltpu.VMEM((EXPERT_DIM, D_MODEL), BF16),
                        pltpu.SemaphoreType.DMA((2,))],
    )
    return pl.pallas_call(
        functools.partial(_expert_kernel, layer=layer),
        grid_spec=grid_spec,
        out_shape=jax.ShapeDtypeStruct((cap, HALF), jnp.uint32),
        compiler_params=_params(("arbitrary",)),
        name="moe_experts",
    )(blk_run, run_e, meta, xs, w1, w3, w2)


def _combine_kernel(dest_ref, wts_ref, h_ref, ys_ref, ws1_ref, ws3_ref, ws2_ref, g_ref, b_ref,
                    o_ref, gbuf, sem, *, tm):
    n = tm * TOP_K

    def row_copy(t, k):
        return pltpu.make_async_copy(ys_ref.at[pl.ds(dest_ref[0, 0, t * TOP_K + k], 1)],
                                     gbuf.at[t, pl.ds(k, 1)], sem)

    def start(t, c):
        for k in range(TOP_K):
            row_copy(t, k).start()
        return c

    def wait(t, c):
        for k in range(TOP_K):
            row_copy(t, k).wait()
        return c

    lax.fori_loop(0, tm, start, 0)
    x = h_ref[...]
    xb = x.astype(BF16)
    h1 = _dg(xb, ws1_ref[...], _NN)
    h3 = _dg(xb, ws3_ref[...], _NN)
    acc = _dg((h1 * _sigmoid(h1) * h3).astype(BF16), ws2_ref[...], _NN)

    expand = ((lax.broadcasted_iota(jnp.int32, (TOP_K, n), 1) & (TOP_K - 1))
              == lax.broadcasted_iota(jnp.int32, (TOP_K, n), 0)).astype(BF16)
    own = (lax.shift_right_logical(lax.broadcasted_iota(jnp.int32, (tm, n), 1), TOP_K.bit_length() - 1)
           == lax.broadcasted_iota(jnp.int32, (tm, n), 0))
    w_hi, w_lo = _split2(wts_ref[...])
    sel_hi = jnp.where(own, _dg(w_hi, expand, _NN), 0.0).astype(BF16)
    sel_lo = jnp.where(own, _dg(w_lo, expand, _NN), 0.0).astype(BF16)

    lax.fori_loop(0, tm, wait, 0)
    y_lo, y_hi = _unpack_rows(gbuf[...].reshape(n, HALF))
    routed = jnp.concatenate([_dg(sel_hi, y_lo, _NN) + _dg(sel_lo, y_lo, _NN),
                              _dg(sel_hi, y_hi, _NN) + _dg(sel_lo, y_hi, _NN)], axis=-1)
    o_ref[...] = _layer_norm(ALPHA * x + (acc + routed), g_ref[...], b_ref[...])


def _combine(h, ys, dest, wts, ws1, ws3, ws2, g, b, tm):
    t = h.shape[0]
    dest3 = dest.reshape(t // tm, 1, tm * TOP_K)
    row = pl.BlockSpec((1, D_MODEL), lambda i: (0, 0))
    full = lambda a: pl.BlockSpec(a.shape, lambda i: (0, 0))
    return pl.pallas_call(
        functools.partial(_combine_kernel, tm=tm),
        grid=(t // tm,),
        in_specs=[pl.BlockSpec((1, 1, tm * TOP_K), lambda i: (i, 0, 0), memory_space=pltpu.SMEM),
                  pl.BlockSpec((tm, TOP_K), lambda i: (i, 0)),
                  pl.BlockSpec((tm, D_MODEL), lambda i: (i, 0)),
                  pl.BlockSpec(memory_space=pl.ANY),
                  full(ws1), full(ws3), full(ws2), row, row],
        out_specs=pl.BlockSpec((tm, D_MODEL), lambda i: (i, 0)),
        out_shape=jax.ShapeDtypeStruct((t, D_MODEL), F32),
        scratch_shapes=[pltpu.VMEM((tm, TOP_K, HALF), jnp.uint32), pltpu.SemaphoreType.DMA(())],
        compiler_params=_params(("arbitrary",)),
        name="moe_combine",
    )(dest3, wts, h, ys, ws1, ws3, ws2, g, b)


def _moe(h, h_packed, router_w, router_bias, w1, w3, w2, layer, ws1, ws3, ws2, ln_g, ln_b):
    t = h.shape[0]
    eidx, pos, wts, counts = _router(h, router_w.T, router_bias.reshape(N_EXPERTS, 1), tm=min(t, 512))
    counts = counts.reshape(N_EXPERTS)
    pcounts = (counts + EXPERT_ROWS - 1) // EXPERT_ROWS * EXPERT_ROWS
    pends = jnp.cumsum(pcounts)
    pstarts = pends - pcounts
    expert_start = jnp.sum(jnp.where(eidx[None] == jnp.arange(N_EXPERTS)[:, None, None],
                                     pstarts[:, None, None], 0), axis=0)
    dest = (expert_start + pos).T.reshape(-1)
    nb = t * TOP_K // EXPERT_ROWS + N_EXPERTS
    blk_start = jnp.arange(nb, dtype=jnp.int32) * EXPERT_ROWS
    blk_e = jnp.minimum(jnp.sum(blk_start[:, None] >= pends[None, :], axis=1), N_EXPERTS - 1).astype(jnp.int32)
    active = counts > 0
    run_of_expert = jnp.cumsum(active.astype(jnp.int32)) - 1
    run_e = jnp.argsort(jnp.logical_not(active), stable=True).astype(jnp.int32)
    blk_run = jnp.sum(jnp.where(blk_e[:, None] == jnp.arange(N_EXPERTS)[None, :], run_of_expert[None, :], 0),
                      axis=1).astype(jnp.int32)
    meta = jnp.stack([pends[-1] // EXPERT_ROWS, jnp.sum(active)]).astype(jnp.int32)
    xs = _dispatch(h_packed, dest, jnp.zeros((nb * EXPERT_ROWS, HALF), jnp.uint32), tm=min(t, 256))
    ys = _experts(xs, blk_run, run_e, meta, w1, w3, w2, layer)
    return _combine(h, ys, dest, wts.T, ws1.astype(BF16), ws3.astype(BF16), ws2.astype(BF16),
                    ln_g.reshape(1, -1), ln_b.reshape(1, -1), tm=min(t, 128))


def _diffattn_kernel(qi_ref, ki_ref, q_ref, k_ref, v_ref, lam_ref, sg_ref, o_ref, qs_ref, m_ref, l_ref, acc_ref,
                     *, tq, lam_init):
    step = pl.program_id(1)
    qi = qi_ref[step]
    ki = ki_ref[step]

    @pl.when(ki == 0)
    def _():
        q = q_ref[...]
        map0 = lax.broadcasted_iota(jnp.int32, q.shape, 1) < DIFF_HEAD_DIM
        zero = jnp.zeros_like(q)
        qs_ref[:tq, :] = jnp.where(map0, q, zero)
        qs_ref[tq:, :] = jnp.where(map0, zero, q)
        m_ref[...] = jnp.full_like(m_ref, -jnp.inf)
        l_ref[...] = jnp.zeros_like(l_ref)
        acc_ref[...] = jnp.zeros_like(acc_ref)

    def accumulate(masked):
        s = _dg(qs_ref[...], k_ref[...], _NT)
        if masked:
            row = lax.broadcasted_iota(jnp.int32, s.shape, 0)
            col = lax.broadcasted_iota(jnp.int32, s.shape, 1)
            s = jnp.where(col <= jnp.where(row >= tq, row - tq, row), s, -jnp.inf)
        chunks = [s[:, c * LANES:(c + 1) * LANES] for c in range(tq // LANES)]
        cmax = chunks[0]
        for x in chunks[1:]:
            cmax = jnp.maximum(cmax, x)
        m_old = m_ref[...]
        m_new = jnp.maximum(m_old, jnp.max(cmax, axis=-1, keepdims=True))
        alpha = jnp.exp2(m_old - m_new)
        ps = [jnp.exp2(x - m_new) for x in chunks]
        psum = ps[0]
        for x in ps[1:]:
            psum = psum + x
        l_ref[...] = alpha * l_ref[...] + psum
        p = jnp.concatenate([x.astype(BF16) for x in ps], axis=-1)
        acc_ref[...] = alpha * acc_ref[...] + _dg(p, v_ref[...], _NN)
        m_ref[...] = m_new

    @pl.when(ki < qi)
    def _():
        accumulate(False)

    @pl.when(ki == qi)
    def _():
        accumulate(True)
        lp = lam_ref[...]
        lam = (jnp.exp(jnp.sum(lp[0:1] * lp[1:2], axis=-1, keepdims=True))
               - jnp.exp(jnp.sum(lp[2:3] * lp[3:4], axis=-1, keepdims=True)) + lam_init)
        l = jnp.sum(l_ref[...], axis=-1, keepdims=True)
        o = acc_ref[:tq, :] / l[:tq, :] - lam * (acc_ref[tq:, :] / l[tq:, :])
        o = o * lax.rsqrt(jnp.mean(o * o, axis=-1, keepdims=True) + RMS_EPS)
        o_ref[...] = o * sg_ref[...] * (1.0 - lam_init)


def _diffattn(q, kv, lam_params, subln_g, lam_init, tq):
    t = q.shape[0]
    nh = DIFF_HEADS
    nq = t // tq
    pairs = [(a, b) for a in range(nq) for b in range(a + 1)]
    qi_tab = jnp.asarray([a for a, _ in pairs], jnp.int32)
    ki_tab = jnp.asarray([b for _, b in pairs], jnp.int32)
    grid_spec = pltpu.PrefetchScalarGridSpec(
        num_scalar_prefetch=2,
        grid=(nh, len(pairs)),
        in_specs=[pl.BlockSpec((tq, LANES), lambda h, s, qt, kt: (qt[s], h)),
                  pl.BlockSpec((tq, LANES), lambda h, s, qt, kt: (kt[s], h)),
                  pl.BlockSpec((tq, LANES), lambda h, s, qt, kt: (kt[s], nh + h)),
                  pl.BlockSpec(lam_params.shape, lambda h, s, qt, kt: (0, 0)),
                  pl.BlockSpec((1, LANES), lambda h, s, qt, kt: (0, 0))],
        out_specs=pl.BlockSpec((tq, LANES), lambda h, s, qt, kt: (qt[s], h)),
        scratch_shapes=[pltpu.VMEM((2 * tq, LANES), BF16),
                        pltpu.VMEM((2 * tq, LANES), F32),
                        pltpu.VMEM((2 * tq, LANES), F32),
                        pltpu.VMEM((2 * tq, LANES), F32)],
    )
    return pl.pallas_call(
        functools.partial(_diffattn_kernel, tq=tq, lam_init=lam_init),
        grid_spec=grid_spec,
        out_shape=jax.ShapeDtypeStruct((t, MAIN_WIDTH), F32),
        compiler_params=_params(("parallel", "arbitrary")),
        name="diff_attn",
    )(qi_tab, ki_tab, q, kv, kv, lam_params, subln_g)


def _pad_rows(w, rows):
    return jnp.pad(w, ((0, rows - w.shape[0]), (0, 0)))


def _layer_a(h, memkv, a_w_in, a_mu, a_w0, a_w_decay_up, a_a0, a_w_a_up, a_w_g_up, a_k_k, a_k_a,
             a_r_k, a_gn_g, a_gn_b, w_out, ln1_g, ln1_b, tm):
    c = MAIN_WIDTH
    lo = 3 * c

    def pad_cols(m):
        pieces = [m[..., :lo],
                  jnp.pad(m[..., lo:lo + DECAY_LORA], ((0, 0), (0, LANES - DECAY_LORA))),
                  jnp.pad(m[..., lo + DECAY_LORA:lo + DECAY_LORA + AAA_LORA], ((0, 0), (0, LANES - AAA_LORA))),
                  jnp.pad(m[..., lo + DECAY_LORA + AAA_LORA:lo + DECAY_LORA + AAA_LORA + GATE_LORA],
                          ((0, 0), (0, 2 * LANES - GATE_LORA))),
                  m[..., lo + DECAY_LORA + AAA_LORA + GATE_LORA:]]
        return jnp.concatenate(pieces, axis=-1)

    w_in = pad_cols(a_w_in).astype(BF16)
    mu_pad = pad_cols(jnp.concatenate([a_mu, jnp.zeros((MEM_WIDTH,), F32)]).reshape(1, -1))[:, :lo + LORA_TILE]
    z = _proj(h, w_in, tm=tm, tn=512, out_dtype=F32)
    lane = jnp.arange(LANES)
    g128 = (lane[:, None] // RWKV_HEAD_DIM == lane[None, :] // RWKV_HEAD_DIM).astype(BF16)
    row = lambda a: a.reshape(1, -1)
    r, lw, k, v, an, b, g = _rwkv_prep(
        z, mu_pad, row(a_w0), _pad_rows(a_w_decay_up, LANES), row(a_a0), _pad_rows(a_w_a_up, LANES),
        _pad_rows(a_w_g_up, 2 * LANES), row(a_k_k), row(a_k_a), g128, tm=min(tm, 256))
    main = _wkv(r, lw, k, v, an, b, g, row(a_r_k), row(a_gn_g), row(a_gn_b), g128)
    memo = _memattn(z, (lo + LORA_TILE) // MEM_WIDTH, memkv, tm=min(tm, 512))
    return _outproj_ln(main, memo, h, w_out.astype(BF16), row(ln1_g), row(ln1_b), tm=min(tm, 256))


def _layer_b(h, memkv, positions, kv_shared_w, b_w_in, lam_params, subln_g, lam_init, w_out, ln1_g, ln1_b, tm):
    c = MAIN_WIDTH
    rope_k = _rope_tables(positions, tm=min(tm, 512), scale=1.0)
    rope_q = _rope_tables(positions, tm=min(tm, 512), scale=DIFF_HEAD_DIM ** -0.5 * math.log2(math.e))
    kv = _proj(h, kv_shared_w.astype(BF16), tm=tm, tn=512, out_dtype=BF16, rope=rope_k, n_rope=c // 512)
    zq = _proj(h, b_w_in.astype(BF16), tm=tm, tn=512, out_dtype=BF16, rope=rope_q, n_rope=c // 512)
    main = _diffattn(zq, kv, lam_params, subln_g.reshape(1, -1), lam_init, tq=min(tm, ATTN_TILE))
    memo = _memattn(zq, c // MEM_WIDTH, memkv, tm=min(tm, 512))
    row = lambda a: a.reshape(1, -1)
    return _outproj_ln(main, memo, h, w_out.astype(BF16), row(ln1_g), row(ln1_b), tm=min(tm, 256))


def kernel(x, mem, positions, a_w_in, a_mu, a_w0, a_w_decay_up, a_a0, a_w_a_up, a_w_g_up, a_k_k, a_k_a, a_r_k, a_gn_g, a_gn_b, kv_shared_w, b_w_in, b_lambda_q1, b_lambda_k1, b_lambda_q2, b_lambda_k2, b_subln_g, mem_w_kv, w_out, ln1_g, ln1_b, ln2_g, ln2_b, router_w, router_bias, exp_w1, exp_w3, exp_w2, sh_w1, sh_w3, sh_w2):
    bsz, t, _ = x.shape
    assert bsz == 1
    h = x.reshape(t, D_MODEL)
    tm = min(t, 1024)
    mem2 = mem.reshape(-1, D_MODEL)
    n_a = DEPTH // 2
    for l in range(DEPTH):
        memkv = _proj(mem2, mem_w_kv[l].astype(BF16), tm=mem2.shape[0], tn=512, out_dtype=F32)
        if l < n_a:
            h, h_packed = _layer_a(h, memkv, a_w_in[l], a_mu[l], a_w0[l], a_w_decay_up[l], a_a0[l], a_w_a_up[l],
                         a_w_g_up[l], a_k_k[l], a_k_a[l], a_r_k[l], a_gn_g[l], a_gn_b[l],
                         w_out[l], ln1_g[l], ln1_b[l], tm)
        else:
            j = l - n_a
            lam_init = 0.8 - 0.6 * math.exp(-0.3 * l)
            lam_params = jnp.stack([b_lambda_q1[j], b_lambda_k1[j], b_lambda_q2[j], b_lambda_k2[j]])
            h, h_packed = _layer_b(h, memkv, positions, kv_shared_w, b_w_in[j], lam_params, b_subln_g[j], lam_init,
                         w_out[l], ln1_g[l], ln1_b[l], tm)
        h = _moe(h, h_packed, router_w[l], router_bias[l], exp_w1, exp_w3, exp_w2, l,
                 sh_w1[l], sh_w3[l], sh_w2[l], ln2_g[l], ln2_b[l])
    return h.reshape(bsz, t, D_MODEL)
```

```python
import functools
import math

import jax
import jax.numpy as jnp
from jax import lax
from jax.experimental import pallas as pl
from jax.experimental.pallas import tpu as pltpu

F32 = jnp.float32
BF16 = jnp.bfloat16

D_MODEL = 2048
DEPTH = 2
MEM_HEADS = 4
MEM_WIDTH = D_MODEL // 4
MEM_HEAD_DIM = MEM_WIDTH // MEM_HEADS
MAIN_WIDTH = D_MODEL - MEM_WIDTH
RWKV_HEAD_DIM = 64
DECAY_LORA = 64
AAA_LORA = 64
GATE_LORA = 224
GN_EPS = 64e-5
DIFF_HEADS = 12
DIFF_HEAD_DIM = 64
ROPE_DIM = DIFF_HEAD_DIM // 4
ROPE_THETA = 500000.0
N_EXPERTS = 64
TOP_K = 8
N_GROUPS = 8
GROUP_SIZE = N_EXPERTS // N_GROUPS
TOPK_GROUPS = 4
EXPERT_DIM = D_MODEL // 4
ROUTED_SCALE = 2.5
ALPHA = (2 * DEPTH) ** 0.25
LN_EPS = 1e-5
RMS_EPS = 1e-5

LANES = 128
VMEM_LIMIT = 56 * 1024 * 1024

LORA_TILE = 512
A_IN_PAD = 3 * MAIN_WIDTH + LORA_TILE + MEM_WIDTH
WKV_CHUNK = 64
WKV_PAIRS = 12
EXPERT_ROWS = 256
ATTN_TILE = 512
COMBINE_GROUP = 32


def _params(sem):
    return pltpu.CompilerParams(dimension_semantics=sem, vmem_limit_bytes=VMEM_LIMIT)


def _split2(x):
    hi = x.astype(BF16)
    lo = (x - hi.astype(F32)).astype(BF16)
    return hi, lo


def _dg(a, b, dims):
    return lax.dot_general(a, b, (dims, ((), ())), preferred_element_type=F32)


_NN = ((1,), (0,))
_NT = ((1,), (1,))
_TN = ((0,), (0,))


def _dot3(a, b, dims=_NN):
    ah, al = _split2(a)
    bh, bl = _split2(b)
    return _dg(ah, bh, dims) + (_dg(ah, bl, dims) + _dg(al, bh, dims))


def _dot2l(a, b_exact, dims=_NN):
    ah, al = _split2(a)
    return _dg(ah, b_exact, dims) + _dg(al, b_exact, dims)


def _dot1(a, b, dims=_NN):
    return _dg(a.astype(BF16), b.astype(BF16), dims)


def _sigmoid(x):
    return 1.0 / (1.0 + jnp.exp(-x))


def _proj_kernel(*refs, n_rope, tn, cast_x):
    if n_rope:
        x_ref, w_ref, c_ref, sa_ref, sb_ref, o_ref = refs[:6]
        rest = refs[6:]
    else:
        x_ref, w_ref, o_ref = refs[:3]
        rest = refs[3:]
    j = pl.program_id(1)
    if cast_x:
        xb_ref, = rest

        @pl.when(j == 0)
        def _():
            xb_ref[...] = x_ref[...].astype(BF16)
        xb = xb_ref[...]
    else:
        xb = x_ref[...]
    acc = jnp.dot(xb, w_ref[...], preferred_element_type=F32)
    if n_rope:
        @pl.when(j < n_rope)
        def _():
            c, sa, sb = c_ref[...], sa_ref[...], sb_ref[...]
            for q in range(tn // LANES):
                a = acc[:, q * LANES:(q + 1) * LANES]
                rot = a * c + pltpu.roll(a, LANES - ROPE_DIM // 2, 1) * sa + pltpu.roll(a, ROPE_DIM // 2, 1) * sb
                o_ref[:, q * LANES:(q + 1) * LANES] = rot.astype(o_ref.dtype)

        @pl.when(j >= n_rope)
        def _():
            o_ref[...] = acc.astype(o_ref.dtype)
    else:
        o_ref[...] = acc.astype(o_ref.dtype)


def _proj(x, w, *, tm, tn, out_dtype, rope=None, n_rope=0):
    m, k = x.shape
    n = w.shape[1]
    assert m % tm == 0 and n % tn == 0
    cast_x = x.dtype != BF16
    in_specs = [pl.BlockSpec((tm, k), lambda i, j: (i, 0)),
                pl.BlockSpec((k, tn), lambda i, j: (0, j))]
    args = [x, w]
    if n_rope:
        in_specs += [pl.BlockSpec((tm, LANES), lambda i, j: (i, 0))] * 3
        args += list(rope)
    return pl.pallas_call(
        functools.partial(_proj_kernel, n_rope=n_rope, tn=tn, cast_x=cast_x),
        grid=(m // tm, n // tn),
        in_specs=in_specs,
        out_specs=pl.BlockSpec((tm, tn), lambda i, j: (i, j)),
        out_shape=jax.ShapeDtypeStruct((m, n), out_dtype),
        scratch_shapes=[pltpu.VMEM((tm, k), BF16)] if cast_x else [],
        compiler_params=_params(("parallel", "arbitrary")),
        name="proj",
    )(*args)


def _rope_table_kernel(pos_ref, invf_ref, c_ref, sa_ref, sb_ref, *, scale):
    ang = pos_ref[...].astype(F32) * invf_ref[...]
    lane = lax.broadcasted_iota(jnp.int32, ang.shape, 1) & (DIFF_HEAD_DIM - 1)
    c = jnp.cos(ang)
    s = jnp.sin(ang)
    half = ROPE_DIM // 2
    c_ref[...] = jnp.where(lane < ROPE_DIM, c, 1.0) * scale
    sa_ref[...] = jnp.where(lane < half, -s, 0.0) * scale
    sb_ref[...] = jnp.where(lane < half, 0.0, jnp.where(lane < ROPE_DIM, s, 0.0)) * scale


def _rope_tables(positions, tm, scale):
    t = positions.shape[1]
    half = ROPE_DIM // 2
    inv_freq = ROPE_THETA ** (-jnp.arange(half, dtype=F32) / half)
    lane = jnp.arange(LANES) % DIFF_HEAD_DIM
    invf = jnp.where(lane < ROPE_DIM, inv_freq[lane % half], 0.0).reshape(1, LANES).astype(F32)
    pos = positions.reshape(t, 1)
    shp = jax.ShapeDtypeStruct((t, LANES), F32)
    return pl.pallas_call(
        functools.partial(_rope_table_kernel, scale=scale),
        grid=(t // tm,),
        in_specs=[pl.BlockSpec((tm, 1), lambda i: (i, 0)), pl.BlockSpec((1, LANES), lambda i: (0, 0))],
        out_specs=[pl.BlockSpec((tm, LANES), lambda i: (i, 0))] * 3,
        out_shape=[shp, shp, shp],
        compiler_params=_params(("parallel",)),
        name="rope_tables",
    )(pos, invf)


def _head_sum(x, g128):
    return _dot2l(x, g128)


def _prep_kernel(z_ref, mu_ref, w0_ref, wdu_ref, a0_ref, wau_ref, wgu_ref, kk_ref, ka_ref, g128_ref,
                 r_o, lw_o, k_o, v_o, an_o, b_o, g_o, carry_ref, *, tm):
    i = pl.program_id(0)

    @pl.when(i == 0)
    def _():
        carry_ref[...] = jnp.zeros_like(carry_ref)

    row0 = lax.broadcasted_iota(jnp.int32, (tm, 1), 0) == 0
    c = MAIN_WIDTH

    def shifted(c0, c1):
        z = z_ref[:, c0:c1]
        prev = jnp.where(row0, carry_ref[:, c0:c1], pltpu.roll(z, 1, 0))
        return z + (prev - z) * mu_ref[:, c0:c1]

    r = shifted(0, c)
    k = shifted(c, 2 * c)
    v = shifted(2 * c, 3 * c)
    lora = shifted(3 * c, 3 * c + LORA_TILE)
    carry_ref[...] = z_ref[pl.ds(tm - 1, 1), :]

    dw = lora[:, 0:LANES]
    da = lora[:, LANES:2 * LANES]
    dg = lora[:, 2 * LANES:4 * LANES]
    wl = w0_ref[...] + _dot3(jnp.tanh(dw), wdu_ref[...])
    lw = -_sigmoid(wl) * math.exp(-0.5)
    a = _sigmoid(a0_ref[...] + _dot3(da, wau_ref[...]))
    g = _dot3(_sigmoid(dg), wgu_ref[...])
    kk = k * kk_ref[...]
    g128 = g128_ref[...]
    for q in range(c // LANES):
        sl = slice(q * LANES, (q + 1) * LANES)
        kq = kk[:, sl]
        nrm = jnp.maximum(jnp.sqrt(_head_sum(kq * kq, g128)), 1e-12)
        kn = kq / nrm
        an_o[:, sl] = -kn
        b_o[:, sl] = kn * a[:, sl]
    r_o[...] = r
    lw_o[...] = lw
    k_o[...] = k * (1.0 + (a - 1.0) * ka_ref[...])
    v_o[...] = v
    g_o[...] = g


def _rwkv_prep(z, mu_pad, w0, wdu_pad, a0, wau_pad, wgu_pad, k_k, k_a, g128, tm):
    t = z.shape[0]
    c = MAIN_WIDTH
    wz = 3 * c + LORA_TILE
    row = lambda n: pl.BlockSpec((1, n), lambda i: (0, 0))
    full = lambda a: pl.BlockSpec(a.shape, lambda i: (0, 0))
    shp = jax.ShapeDtypeStruct((t, c), F32)
    return pl.pallas_call(
        functools.partial(_prep_kernel, tm=tm),
        grid=(t // tm,),
        in_specs=[pl.BlockSpec((tm, wz), lambda i: (i, 0)), row(wz), row(c), full(wdu_pad), row(c),
                  full(wau_pad), full(wgu_pad), row(c), row(c), full(g128)],
        out_specs=[pl.BlockSpec((tm, c), lambda i: (i, 0))] * 7,
        out_shape=[shp] * 7,
        scratch_shapes=[pltpu.VMEM((1, wz), F32)],
        compiler_params=_params(("arbitrary",)),
        name="rwkv_prep",
    )(z, mu_pad, w0, wdu_pad, a0, wau_pad, wgu_pad, k_k, k_a, g128)


def _wkv_kernel(r_ref, lw_ref, k_ref, v_ref, an_ref, b_ref, g_ref, rk_ref, gng_ref, gnb_ref, g128_ref,
                o_ref, ht_ref, *, L, pairs):
    ci = pl.program_id(1)

    @pl.when(ci == 0)
    def _():
        ht_ref[...] = jnp.zeros_like(ht_ref)

    P = range(pairs)
    cols = [slice(p * LANES, (p + 1) * LANES) for p in P]
    ld = lambda ref: [ref[:, c] for c in cols]
    r, lw, k, v, an, b = ld(r_ref), ld(lw_ref), ld(k_ref), ld(v_ref), ld(an_ref), ld(b_ref)
    g128 = g128_ref[...]
    n2 = 2 * L

    rowi = lax.broadcasted_iota(jnp.int32, (L, L), 0)
    coli = lax.broadcasted_iota(jnp.int32, (L, L), 1)
    tri = (coli <= rowi).astype(BF16)

    def running_sum(x):
        x_h = x.astype(BF16)
        x_r = x - x_h.astype(F32)
        x_m = x_r.astype(BF16)
        x_l = (x_r - x_m.astype(F32)).astype(BF16)
        return _dg(tri, x_h, _NN) + (_dg(tri, x_m, _NN) + _dg(tri, x_l, _NN))

    cum = [running_sum(lw[p]) for p in P]
    cum_last = [cum[p][L - 1:L, :] for p in P]
    p_inv = [jnp.exp(-cum[p]) for p in P]
    at = [an[p] * jnp.exp(cum[p] - lw[p]) for p in P]
    rt = [r[p] * jnp.exp(cum[p]) for p in P]
    bt = [b[p] * p_inv[p] for p in P]
    kt = [k[p] * p_inv[p] for p in P]
    p_end = [jnp.exp(cum_last[p] - cum[p]) for p in P]

    head0 = lax.broadcasted_iota(jnp.int32, (L, LANES), 1) < RWKV_HEAD_DIM

    def stack(x):
        return jnp.concatenate([jnp.where(head0, x, 0.0), jnp.where(head0, 0.0, x)], axis=0)

    ar = [jnp.concatenate([stack(at[p]), stack(rt[p])], axis=0) for p in P]
    bk = [jnp.concatenate([stack(bt[p]), stack(kt[p])], axis=0) for p in P]
    v_st = [stack(v[p]) for p in P]
    m = [_dot3(ar[p], bk[p], _NT) for p in P]

    ri = lax.broadcasted_iota(jnp.int32, (n2, n2), 0)
    cj = lax.broadcasted_iota(jnp.int32, (n2, n2), 1)
    same = (ri >= L) == (cj >= L)
    strict = same & (cj < ri)
    incl = same & (cj <= ri)
    ab = [jnp.where(strict, m[p][:n2, :n2], 0.0) for p in P]
    akrk = [jnp.concatenate([jnp.where(strict, m[p][:n2, n2:], 0.0),
                             jnp.where(incl, m[p][n2:, n2:], 0.0)], axis=0) for p in P]
    rb = [jnp.where(incl, m[p][n2:, :n2], 0.0) for p in P]

    ht = [ht_ref[p] for p in P]
    arh = [_dot2l(ar[p], ht[p].astype(BF16), _NT) for p in P]
    kv = [_dot1(akrk[p], v_st[p]) for p in P]
    u = [arh[p][:n2] + kv[p][:n2] for p in P]
    apow = ab
    steps = max(1, (L - 1).bit_length())
    for s in range(steps):
        u = [u[p] + _dot1(apow[p], u[p]) for p in P]
        if s + 1 < steps:
            apow = [_dot1(apow[p], apow[p]) for p in P]
    y_st = [arh[p][n2:] + kv[p][n2:] + _dot1(rb[p], u[p]) for p in P]
    y = [y_st[p][:L] + y_st[p][L:] for p in P]

    bk_end = [jnp.concatenate([stack(b[p] * p_end[p]), stack(k[p] * p_end[p])], axis=0) for p in P]
    for p in P:
        uv = jnp.concatenate([u[p], v_st[p]], axis=0)
        ht_ref[p] = ht[p] * jnp.exp(cum_last[p]) + _dot1(uv, bk_end[p], _TN)

    inv_n = 1.0 / RWKV_HEAD_DIM
    mu = [_head_sum(y[p], g128) * inv_n for p in P]
    yc = [y[p] - mu[p] for p in P]
    var = [_head_sum(yc[p] * yc[p], g128) * inv_n for p in P]
    bonus = [_head_sum(r[p] * k[p] * rk_ref[:, cols[p]], g128) * v[p] for p in P]
    for p in P:
        yn = yc[p] * lax.rsqrt(var[p] + GN_EPS) * gng_ref[:, cols[p]] + gnb_ref[:, cols[p]]
        o_ref[:, cols[p]] = (yn + bonus[p]) * g_ref[:, cols[p]]


def _wkv(r, lw, k, v, an, b, g, r_k, gn_g, gn_b, g128):
    t, c = r.shape
    L = WKV_CHUNK
    w = WKV_PAIRS * LANES
    tok = pl.BlockSpec((L, w), lambda p, ci: (ci, p))
    par = pl.BlockSpec((1, w), lambda p, ci: (0, p))
    return pl.pallas_call(
        functools.partial(_wkv_kernel, L=L, pairs=WKV_PAIRS),
        grid=(c // w, t // L),
        in_specs=[tok] * 7 + [par] * 3 + [pl.BlockSpec((LANES, LANES), lambda p, ci: (0, 0))],
        out_specs=tok,
        out_shape=jax.ShapeDtypeStruct((t, c), F32),
        scratch_shapes=[pltpu.VMEM((WKV_PAIRS, LANES, LANES), F32)],
        compiler_params=_params(("parallel", "arbitrary")),
        name="wkv7",
    )(r, lw, k, v, an, b, g, r_k, gn_g, gn_b, g128)


def _memattn_kernel(q_ref, kv_ref, o_ref):
    scale = MEM_HEAD_DIM ** -0.5
    for hh in range(MEM_HEADS):
        sl = slice(hh * MEM_HEAD_DIM, (hh + 1) * MEM_HEAD_DIM)
        q = q_ref[:, sl].astype(BF16)
        kh = kv_ref[:, sl].astype(BF16)
        vh = kv_ref[:, MEM_WIDTH + hh * MEM_HEAD_DIM:MEM_WIDTH + (hh + 1) * MEM_HEAD_DIM].astype(BF16)
        s = _dg(q, kh, _NT) * scale
        s = s - jnp.max(s, axis=-1, keepdims=True)
        e = jnp.exp(s)
        p = e / jnp.sum(e, axis=-1, keepdims=True)
        o_ref[:, sl] = _dg(p.astype(BF16), vh, _NN)


def _memattn(zq, q_col_block, memkv, tm):
    t = zq.shape[0]
    return pl.pallas_call(
        _memattn_kernel,
        grid=(t // tm,),
        in_specs=[pl.BlockSpec((tm, MEM_WIDTH), lambda i: (i, q_col_block)),
                  pl.BlockSpec(memkv.shape, lambda i: (0, 0))],
        out_specs=pl.BlockSpec((tm, MEM_WIDTH), lambda i: (i, 0)),
        out_shape=jax.ShapeDtypeStruct((t, MEM_WIDTH), F32),
        compiler_params=_params(("parallel",)),
        name="mem_attn",
    )(zq, memkv)


def _layer_norm(x, g, b):
    mu = jnp.mean(x, axis=-1, keepdims=True)
    xc = x - mu
    var = jnp.mean(xc * xc, axis=-1, keepdims=True)
    return xc * lax.rsqrt(var + LN_EPS) * g + b


HALF = D_MODEL // 2


def _pack_rows(x):
    lo = pltpu.bitcast(x[:, :HALF].astype(BF16).astype(F32), jnp.uint32)
    hi = pltpu.bitcast(x[:, HALF:].astype(BF16).astype(F32), jnp.uint32)
    return hi | lax.shift_right_logical(lo, jnp.uint32(16))


def _unpack_rows(p):
    lo = pltpu.bitcast(lax.shift_left(p, jnp.uint32(16)), F32).astype(BF16)
    hi = pltpu.bitcast(p & jnp.uint32(0xFFFF0000), F32).astype(BF16)
    return lo, hi


def _outproj_kernel(main_ref, memo_ref, h_ref, w_ref, g_ref, b_ref, o_ref, op_ref):
    mix = (_dg(main_ref[...].astype(BF16), w_ref[:MAIN_WIDTH, :], _NN)
           + _dg(memo_ref[...].astype(BF16), w_ref[MAIN_WIDTH:, :], _NN))
    y = _layer_norm(ALPHA * h_ref[...] + mix, g_ref[...], b_ref[...])
    o_ref[...] = y
    op_ref[...] = _pack_rows(y)


def _outproj_ln(main, memo, h, w_bf, g, b, tm):
    t = h.shape[0]
    row = pl.BlockSpec((1, D_MODEL), lambda i: (0, 0))
    return pl.pallas_call(
        _outproj_kernel,
        grid=(t // tm,),
        in_specs=[pl.BlockSpec((tm, MAIN_WIDTH), lambda i: (i, 0)),
                  pl.BlockSpec((tm, MEM_WIDTH), lambda i: (i, 0)),
                  pl.BlockSpec((tm, D_MODEL), lambda i: (i, 0)),
                  pl.BlockSpec((D_MODEL, D_MODEL), lambda i: (0, 0)), row, row],
        out_specs=[pl.BlockSpec((tm, D_MODEL), lambda i: (i, 0)),
                   pl.BlockSpec((tm, HALF), lambda i: (i, 0))],
        out_shape=[jax.ShapeDtypeStruct((t, D_MODEL), F32),
                   jax.ShapeDtypeStruct((t, HALF), jnp.uint32)],
        compiler_params=_params(("parallel",)),
        name="outproj_ln",
    )(main, memo, h, w_bf, g, b)


def _first_max(vals, ids):
    m = vals[0]
    for x in vals[1:]:
        m = jnp.maximum(m, x)
    m = jnp.max(m, axis=0, keepdims=True)
    big = jnp.int32(1 << 20)
    idx = None
    for x, e in zip(vals, ids):
        cand = jnp.where(x == m, e, big)
        idx = cand if idx is None else jnp.minimum(idx, cand)
    idx = jnp.min(idx, axis=0, keepdims=True)
    return m, idx


def _router_kernel(h_ref, wt_ref, bias_ref, eidx_ref, pos_ref, wts_ref, cnt_ref, run_ref, *, tm):
    i = pl.program_id(0)

    @pl.when(i == 0)
    def _():
        run_ref[...] = jnp.zeros_like(run_ref)

    neg = -jnp.inf
    logits = _dot3(wt_ref[...], h_ref[...], _NT)
    scores = _sigmoid(logits)
    biased = scores + bias_ref[...]
    sub = lax.broadcasted_iota(jnp.int32, (GROUP_SIZE, tm), 0)
    grp = [biased[g * GROUP_SIZE:(g + 1) * GROUP_SIZE, :] for g in range(N_GROUPS)]
    sc = [scores[g * GROUP_SIZE:(g + 1) * GROUP_SIZE, :] for g in range(N_GROUPS)]
    ids = [sub + g * GROUP_SIZE for g in range(N_GROUPS)]

    gs = []
    for g in range(N_GROUPS):
        m1, i1 = _first_max([grp[g]], [sub])
        m2 = jnp.max(jnp.where(sub == i1, neg, grp[g]), axis=0, keepdims=True)
        gs.append(m1 + m2)
    gscore = jnp.concatenate(gs, axis=0)
    gmask = jnp.zeros((N_GROUPS, tm), jnp.bool_)
    cur = gscore
    for _ in range(TOPK_GROUPS):
        _, gi = _first_max([cur], [sub])
        hit = sub == gi
        gmask = gmask | hit
        cur = jnp.where(hit, neg, cur)
    gm = gmask.astype(F32)
    cur = [jnp.where(gm[g:g + 1, :] > 0.0, grp[g], neg) for g in range(N_GROUPS)]

    sel = [jnp.zeros((GROUP_SIZE, tm), jnp.bool_) for _ in range(N_GROUPS)]
    e_rows, w_rows = [], []
    for _ in range(TOP_K):
        _, ei = _first_max(cur, ids)
        w = jnp.zeros((1, tm), F32)
        for g in range(N_GROUPS):
            hit = ids[g] == ei
            sel[g] = sel[g] | hit
            cur[g] = jnp.where(hit, neg, cur[g])
            w = w + jnp.sum(jnp.where(hit, sc[g], 0.0), axis=0, keepdims=True)
        e_rows.append(ei)
        w_rows.append(w)
    eidx = jnp.concatenate(e_rows, axis=0)
    wraw = jnp.concatenate(w_rows, axis=0)
    wts_ref[...] = wraw / (jnp.sum(wraw, axis=0, keepdims=True) + 1e-20) * ROUTED_SCALE
    eidx_ref[...] = eidx

    self32 = jnp.concatenate([s.astype(F32) for s in sel], axis=0)
    ui = lax.broadcasted_iota(jnp.int32, (tm, tm), 0)
    uj = lax.broadcasted_iota(jnp.int32, (tm, tm), 1)
    before = (ui < uj).astype(BF16)
    local = _dg(self32.astype(BF16), before, _NN)
    run = run_ref[...]
    glob = local + run
    run_new = run + jnp.sum(self32, axis=1, keepdims=True)
    run_ref[...] = run_new
    cnt_ref[...] = run_new.astype(jnp.int32)
    e64 = lax.broadcasted_iota(jnp.int32, (N_EXPERTS, tm), 0)
    p_rows = [jnp.sum(jnp.where(e64 == e_rows[k], glob, 0.0), axis=0, keepdims=True) for k in range(TOP_K)]
    pos_ref[...] = jnp.concatenate(p_rows, axis=0).astype(jnp.int32)


def _router(h, w_router_t, bias_col, tm):
    t = h.shape[0]
    return pl.pallas_call(
        functools.partial(_router_kernel, tm=tm),
        grid=(t // tm,),
        in_specs=[pl.BlockSpec((tm, D_MODEL), lambda i: (i, 0)),
                  pl.BlockSpec((N_EXPERTS, D_MODEL), lambda i: (0, 0)),
                  pl.BlockSpec((N_EXPERTS, 1), lambda i: (0, 0))],
        out_specs=[pl.BlockSpec((TOP_K, tm), lambda i: (0, i)),
                   pl.BlockSpec((TOP_K, tm), lambda i: (0, i)),
                   pl.BlockSpec((TOP_K, tm), lambda i: (0, i)),
                   pl.BlockSpec((N_EXPERTS, 1), lambda i: (0, 0))],
        out_shape=[jax.ShapeDtypeStruct((TOP_K, t), jnp.int32),
                   jax.ShapeDtypeStruct((TOP_K, t), jnp.int32),
                   jax.ShapeDtypeStruct((TOP_K, t), F32),
                   jax.ShapeDtypeStruct((N_EXPERTS, 1), jnp.int32)],
        scratch_shapes=[pltpu.VMEM((N_EXPERTS, 1), F32)],
        compiler_params=_params(("arbitrary",)),
        name="router",
    )(h, w_router_t, bias_col)


def _dispatch_kernel(dest_ref, h_ref, xs_in_ref, xs_ref, sem, *, tm):
    del xs_in_ref

    def row_copy(t, k):
        return pltpu.make_async_copy(h_ref.at[pl.ds(t, 1)],
                                     xs_ref.at[pl.ds(dest_ref[0, 0, t * TOP_K + k], 1)], sem)

    def start(t, c):
        for k in range(TOP_K):
            row_copy(t, k).start()
        return c

    def wait(t, c):
        for k in range(TOP_K):
            row_copy(t, k).wait()
        return c

    lax.fori_loop(0, tm, start, 0)
    lax.fori_loop(0, tm, wait, 0)


def _dispatch(h, dest, xs_init, tm):
    t = h.shape[0]
    dest3 = dest.reshape(t // tm, 1, tm * TOP_K)
    return pl.pallas_call(
        functools.partial(_dispatch_kernel, tm=tm),
        grid=(t // tm,),
        in_specs=[pl.BlockSpec((1, 1, tm * TOP_K), lambda i: (i, 0, 0), memory_space=pltpu.SMEM),
                  pl.BlockSpec((tm, h.shape[1]), lambda i: (i, 0)),
                  pl.BlockSpec(memory_space=pl.ANY)],
        out_specs=pl.BlockSpec(memory_space=pl.ANY),
        out_shape=jax.ShapeDtypeStruct(xs_init.shape, xs_init.dtype),
        scratch_shapes=[pltpu.SemaphoreType.DMA(())],
        input_output_aliases={2: 0},
        compiler_params=_params(("arbitrary",)),
        name="moe_dispatch",
    )(dest3, h, xs_init)


def _expert_kernel(blk_run_ref, run_e_ref, meta_ref, x_ref, w1_hbm, w3_hbm, w2_hbm, y_ref,
                   w1f, w3f, w2f, w1b, w3b, w2b, sem, *, layer):
    b = pl.program_id(0)
    n_used, n_runs = meta_ref[0], meta_ref[1]
    used = b < n_used
    run = blk_run_ref[b]
    fresh = jnp.logical_or(b == 0, run != blk_run_ref[jnp.maximum(b - 1, 0)])

    def fetch(r, slot):
        e = run_e_ref[r]
        return (pltpu.make_async_copy(w1_hbm.at[layer, e], w1f.at[slot], sem.at[slot]),
                pltpu.make_async_copy(w3_hbm.at[layer, e], w3f.at[slot], sem.at[slot]),
                pltpu.make_async_copy(w2_hbm.at[layer, e], w2f.at[slot], sem.at[slot]))

    @pl.when(jnp.logical_and(used, fresh))
    def _():
        slot = run & 1

        @pl.when(b == 0)
        def _():
            for c in fetch(run, slot):
                c.start()

        @pl.when(run + 1 < n_runs)
        def _():
            for c in fetch(run + 1, 1 - slot):
                c.start()

        for c in fetch(run, slot):
            c.wait()
        w1b[...] = w1f[slot].astype(BF16)
        w3b[...] = w3f[slot].astype(BF16)
        w2b[...] = w2f[slot].astype(BF16)

    @pl.when(used)
    def _():
        x_lo, x_hi = _unpack_rows(x_ref[...])
        h1 = _dg(x_lo, w1b[:HALF, :], _NN) + _dg(x_hi, w1b[HALF:, :], _NN)
        h3 = _dg(x_lo, w3b[:HALF, :], _NN) + _dg(x_hi, w3b[HALF:, :], _NN)
        hb = (h1 * _sigmoid(h1) * h3).astype(BF16)
        y_ref[...] = _pack_rows(_dg(hb, w2b[...], _NN))

    @pl.when(jnp.logical_not(used))
    def _():
        y_ref[...] = jnp.zeros_like(y_ref)


def _experts(xs, blk_run, run_e, meta, w1, w3, w2, layer):
    cap = xs.shape[0]
    nb = cap // EXPERT_ROWS
    xmap = lambda b, br, re, mt: (jnp.minimum(b, mt[0] - 1), 0)
    hbm = pl.BlockSpec(memory_space=pl.ANY)
    grid_spec = pltpu.PrefetchScalarGridSpec(
        num_scalar_prefetch=3,
        grid=(nb,),
        in_specs=[pl.BlockSpec((EXPERT_ROWS, HALF), xmap), hbm, hbm, hbm],
        out_specs=pl.BlockSpec((EXPERT_ROWS, HALF), lambda b, br, re, mt: (b, 0)),
        scratch_shapes=[pltpu.VMEM((2, D_MODEL, EXPERT_DIM), F32),
                        pltpu.VMEM((2, D_MODEL, EXPERT_DIM), F32),
                        pltpu.VMEM((2, EXPERT_DIM, D_MODEL), F32),
                        pltpu.VMEM((D_MODEL, EXPERT_DIM), BF16),
                        pltpu.VMEM((D_MODEL, EXPERT_DIM), BF16),
                        pltpu.VMEM((EXPERT_DIM, D_MODEL), BF16),
                        pltpu.SemaphoreType.DMA((2,))],
    )
    return pl.pallas_call(
        functools.partial(_expert_kernel, layer=layer),
        grid_spec=grid_spec,
        out_shape=jax.ShapeDtypeStruct((cap, HALF), jnp.uint32),
        compiler_params=_params(("arbitrary",)),
        name="moe_experts",
    )(blk_run, run_e, meta, xs, w1, w3, w2)


def _combine_kernel(dest_ref, wts_ref, h_ref, ys_ref, ws1_ref, ws3_ref, ws2_ref, g_ref, b_ref,
                    o_ref, gbuf, sem, *, tm):
    n = tm * TOP_K

    def row_copy(t, k):
        return pltpu.make_async_copy(ys_ref.at[pl.ds(dest_ref[0, 0, t * TOP_K + k], 1)],
                                     gbuf.at[t, pl.ds(k, 1)], sem)

    def start(t, c):
        for k in range(TOP_K):
            row_copy(t, k).start()
        return c

    def wait(t, c):
        for k in range(TOP_K):
            row_copy(t, k).wait()
        return c

    lax.fori_loop(0, tm, start, 0)
    x = h_ref[...]
    xb = x.astype(BF16)
    h1 = _dg(xb, ws1_ref[...], _NN)
    h3 = _dg(xb, ws3_ref[...], _NN)
    acc = _dg((h1 * _sigmoid(h1) * h3).astype(BF16), ws2_ref[...], _NN)

    expand = ((lax.broadcasted_iota(jnp.int32, (TOP_K, n), 1) & (TOP_K - 1))
              == lax.broadcasted_iota(jnp.int32, (TOP_K, n), 0)).astype(BF16)
    own = (lax.shift_right_logical(lax.broadcasted_iota(jnp.int32, (tm, n), 1), TOP_K.bit_length() - 1)
           == lax.broadcasted_iota(jnp.int32, (tm, n), 0))
    w_hi, w_lo = _split2(wts_ref[...])
    sel_hi = jnp.where(own, _dg(w_hi, expand, _NN), 0.0).astype(BF16)
    sel_lo = jnp.where(own, _dg(w_lo, expand, _NN), 0.0).astype(BF16)

    lax.fori_loop(0, tm, wait, 0)
    y_lo, y_hi = _unpack_rows(gbuf[...].reshape(n, HALF))
    grp = min(tm, COMBINE_GROUP)
    parts = []
    for g0 in range(0, tm, grp):
        rows, cols = slice(g0, g0 + grp), slice(g0 * TOP_K, (g0 + grp) * TOP_K)
        sub = jnp.concatenate([sel_hi[rows, cols], sel_lo[rows, cols]], axis=0)
        r_lo = _dg(sub, y_lo[cols], _NN)
        r_hi = _dg(sub, y_hi[cols], _NN)
        parts.append(jnp.concatenate([r_lo[:grp] + r_lo[grp:], r_hi[:grp] + r_hi[grp:]], axis=-1))
    routed = jnp.concatenate(parts, axis=0)
    o_ref[...] = _layer_norm(ALPHA * x + (acc + routed), g_ref[...], b_ref[...])


def _combine(h, ys, dest, wts, ws1, ws3, ws2, g, b, tm):
    t = h.shape[0]
    dest3 = dest.reshape(t // tm, 1, tm * TOP_K)
    row = pl.BlockSpec((1, D_MODEL), lambda i: (0, 0))
    full = lambda a: pl.BlockSpec(a.shape, lambda i: (0, 0))
    return pl.pallas_call(
        functools.partial(_combine_kernel, tm=tm),
        grid=(t // tm,),
        in_specs=[pl.BlockSpec((1, 1, tm * TOP_K), lambda i: (i, 0, 0), memory_space=pltpu.SMEM),
                  pl.BlockSpec((tm, TOP_K), lambda i: (i, 0)),
                  pl.BlockSpec((tm, D_MODEL), lambda i: (i, 0)),
                  pl.BlockSpec(memory_space=pl.ANY),
                  full(ws1), full(ws3), full(ws2), row, row],
        out_specs=pl.BlockSpec((tm, D_MODEL), lambda i: (i, 0)),
        out_shape=jax.ShapeDtypeStruct((t, D_MODEL), F32),
        scratch_shapes=[pltpu.VMEM((tm, TOP_K, HALF), jnp.uint32), pltpu.SemaphoreType.DMA(())],
        compiler_params=_params(("arbitrary",)),
        name="moe_combine",
    )(dest3, wts, h, ys, ws1, ws3, ws2, g, b)


def _moe(h, h_packed, router_w, router_bias, w1, w3, w2, layer, ws1, ws3, ws2, ln_g, ln_b):
    t = h.shape[0]
    eidx, pos, wts, counts = _router(h, router_w.T, router_bias.reshape(N_EXPERTS, 1), tm=min(t, 512))
    counts = counts.reshape(N_EXPERTS)
    pcounts = (counts + EXPERT_ROWS - 1) // EXPERT_ROWS * EXPERT_ROWS
    pends = jnp.cumsum(pcounts)
    pstarts = pends - pcounts
    expert_start = jnp.sum(jnp.where(eidx[None] == jnp.arange(N_EXPERTS)[:, None, None],
                                     pstarts[:, None, None], 0), axis=0)
    dest = (expert_start + pos).T.reshape(-1)
    nb = t * TOP_K // EXPERT_ROWS + N_EXPERTS
    blk_start = jnp.arange(nb, dtype=jnp.int32) * EXPERT_ROWS
    blk_e = jnp.minimum(jnp.sum(blk_start[:, None] >= pends[None, :], axis=1), N_EXPERTS - 1).astype(jnp.int32)
    active = counts > 0
    run_of_expert = jnp.cumsum(active.astype(jnp.int32)) - 1
    run_e = jnp.argsort(jnp.logical_not(active), stable=True).astype(jnp.int32)
    blk_run = jnp.sum(jnp.where(blk_e[:, None] == jnp.arange(N_EXPERTS)[None, :], run_of_expert[None, :], 0),
                      axis=1).astype(jnp.int32)
    meta = jnp.stack([pends[-1] // EXPERT_ROWS, jnp.sum(active)]).astype(jnp.int32)
    xs = _dispatch(h_packed, dest, jnp.zeros((nb * EXPERT_ROWS, HALF), jnp.uint32), tm=min(t, 256))
    ys = _experts(xs, blk_run, run_e, meta, w1, w3, w2, layer)
    return _combine(h, ys, dest, wts.T, ws1.astype(BF16), ws3.astype(BF16), ws2.astype(BF16),
                    ln_g.reshape(1, -1), ln_b.reshape(1, -1), tm=min(t, 128))


def _diffattn_kernel(qi_ref, ki_ref, q_ref, k_ref, v_ref, lam_ref, sg_ref, o_ref, qs_ref, m_ref, l_ref, acc_ref,
                     *, tq, lam_init):
    step = pl.program_id(1)
    qi = qi_ref[step]
    ki = ki_ref[step]

    @pl.when(ki == 0)
    def _():
        q = q_ref[...]
        map0 = lax.broadcasted_iota(jnp.int32, q.shape, 1) < DIFF_HEAD_DIM
        zero = jnp.zeros_like(q)
        qs_ref[:tq, :] = jnp.where(map0, q, zero)
        qs_ref[tq:, :] = jnp.where(map0, zero, q)
        m_ref[...] = jnp.full_like(m_ref, -jnp.inf)
        l_ref[...] = jnp.zeros_like(l_ref)
        acc_ref[...] = jnp.zeros_like(acc_ref)

    def accumulate(masked):
        s = _dg(qs_ref[...], k_ref[...], _NT)
        if masked:
            row = lax.broadcasted_iota(jnp.int32, s.shape, 0)
            col = lax.broadcasted_iota(jnp.int32, s.shape, 1)
            s = jnp.where(col <= jnp.where(row >= tq, row - tq, row), s, -jnp.inf)
        chunks = [s[:, c * LANES:(c + 1) * LANES] for c in range(tq // LANES)]
        cmax = chunks[0]
        for x in chunks[1:]:
            cmax = jnp.maximum(cmax, x)
        m_old = m_ref[...]
        m_new = jnp.maximum(m_old, jnp.max(cmax, axis=-1, keepdims=True))
        alpha = jnp.exp2(m_old - m_new)
        ps = [jnp.exp2(x - m_new) for x in chunks]
        psum = ps[0]
        for x in ps[1:]:
            psum = psum + x
        l_ref[...] = alpha * l_ref[...] + psum
        p = jnp.concatenate([x.astype(BF16) for x in ps], axis=-1)
        acc_ref[...] = alpha * acc_ref[...] + _dg(p, v_ref[...], _NN)
        m_ref[...] = m_new

    @pl.when(ki < qi)
    def _():
        accumulate(False)

    @pl.when(ki == qi)
    def _():
        accumulate(True)
        lp = lam_ref[...]
        lam = (jnp.exp(jnp.sum(lp[0:1] * lp[1:2], axis=-1, keepdims=True))
               - jnp.exp(jnp.sum(lp[2:3] * lp[3:4], axis=-1, keepdims=True)) + lam_init)
        l = jnp.sum(l_ref[...], axis=-1, keepdims=True)
        o = acc_ref[:tq, :] / l[:tq, :] - lam * (acc_ref[tq:, :] / l[tq:, :])
        o = o * lax.rsqrt(jnp.mean(o * o, axis=-1, keepdims=True) + RMS_EPS)
        o_ref[...] = o * sg_ref[...] * (1.0 - lam_init)


def _diffattn(q, kv, lam_params, subln_g, lam_init, tq):
    t = q.shape[0]
    nh = DIFF_HEADS
    nq = t // tq
    pairs = [(a, b) for a in range(nq) for b in range(a + 1)]
    qi_tab = jnp.asarray([a for a, _ in pairs], jnp.int32)
    ki_tab = jnp.asarray([b for _, b in pairs], jnp.int32)
    grid_spec = pltpu.PrefetchScalarGridSpec(
        num_scalar_prefetch=2,
        grid=(nh, len(pairs)),
        in_specs=[pl.BlockSpec((tq, LANES), lambda h, s, qt, kt: (qt[s], h)),
                  pl.BlockSpec((tq, LANES), lambda h, s, qt, kt: (kt[s], h)),
                  pl.BlockSpec((tq, LANES), lambda h, s, qt, kt: (kt[s], nh + h)),
                  pl.BlockSpec(lam_params.shape, lambda h, s, qt, kt: (0, 0)),
                  pl.BlockSpec((1, LANES), lambda h, s, qt, kt: (0, 0))],
        out_specs=pl.BlockSpec((tq, LANES), lambda h, s, qt, kt: (qt[s], h)),
        scratch_shapes=[pltpu.VMEM((2 * tq, LANES), BF16),
                        pltpu.VMEM((2 * tq, LANES), F32),
                        pltpu.VMEM((2 * tq, LANES), F32),
                        pltpu.VMEM((2 * tq, LANES), F32)],
    )
    return pl.pallas_call(
        functools.partial(_diffattn_kernel, tq=tq, lam_init=lam_init),
        grid_spec=grid_spec,
        out_shape=jax.ShapeDtypeStruct((t, MAIN_WIDTH), F32),
        compiler_params=_params(("parallel", "arbitrary")),
        name="diff_attn",
    )(qi_tab, ki_tab, q, kv, kv, lam_params, subln_g)


def _pad_rows(w, rows):
    return jnp.pad(w, ((0, rows - w.shape[0]), (0, 0)))


def _layer_a(h, memkv, a_w_in, a_mu, a_w0, a_w_decay_up, a_a0, a_w_a_up, a_w_g_up, a_k_k, a_k_a,
             a_r_k, a_gn_g, a_gn_b, w_out, ln1_g, ln1_b, tm):
    c = MAIN_WIDTH
    lo = 3 * c

    def pad_cols(m):
        pieces = [m[..., :lo],
                  jnp.pad(m[..., lo:lo + DECAY_LORA], ((0, 0), (0, LANES - DECAY_LORA))),
                  jnp.pad(m[..., lo + DECAY_LORA:lo + DECAY_LORA + AAA_LORA], ((0, 0), (0, LANES - AAA_LORA))),
                  jnp.pad(m[..., lo + DECAY_LORA + AAA_LORA:lo + DECAY_LORA + AAA_LORA + GATE_LORA],
                          ((0, 0), (0, 2 * LANES - GATE_LORA))),
                  m[..., lo + DECAY_LORA + AAA_LORA + GATE_LORA:]]
        return jnp.concatenate(pieces, axis=-1)

    w_in = pad_cols(a_w_in).astype(BF16)
    mu_pad = pad_cols(jnp.concatenate([a_mu, jnp.zeros((MEM_WIDTH,), F32)]).reshape(1, -1))[:, :lo + LORA_TILE]
    z = _proj(h, w_in, tm=tm, tn=512, out_dtype=F32)
    lane = jnp.arange(LANES)
    g128 = (lane[:, None] // RWKV_HEAD_DIM == lane[None, :] // RWKV_HEAD_DIM).astype(BF16)
    row = lambda a: a.reshape(1, -1)
    r, lw, k, v, an, b, g = _rwkv_prep(
        z, mu_pad, row(a_w0), _pad_rows(a_w_decay_up, LANES), row(a_a0), _pad_rows(a_w_a_up, LANES),
        _pad_rows(a_w_g_up, 2 * LANES), row(a_k_k), row(a_k_a), g128, tm=min(tm, 256))
    main = _wkv(r, lw, k, v, an, b, g, row(a_r_k), row(a_gn_g), row(a_gn_b), g128)
    memo = _memattn(z, (lo + LORA_TILE) // MEM_WIDTH, memkv, tm=min(tm, 512))
    return _outproj_ln(main, memo, h, w_out.astype(BF16), row(ln1_g), row(ln1_b), tm=min(tm, 256))


def _layer_b(h, memkv, positions, kv_shared_w, b_w_in, lam_params, subln_g, lam_init, w_out, ln1_g, ln1_b, tm):
    c = MAIN_WIDTH
    rope_k = _rope_tables(positions, tm=min(tm, 512), scale=1.0)
    rope_q = _rope_tables(positions, tm=min(tm, 512), scale=DIFF_HEAD_DIM ** -0.5 * math.log2(math.e))
    kv = _proj(h, kv_shared_w.astype(BF16), tm=tm, tn=512, out_dtype=BF16, rope=rope_k, n_rope=c // 512)
    zq = _proj(h, b_w_in.astype(BF16), tm=tm, tn=512, out_dtype=BF16, rope=rope_q, n_rope=c // 512)
    main = _diffattn(zq, kv, lam_params, subln_g.reshape(1, -1), lam_init, tq=min(tm, ATTN_TILE))
    memo = _memattn(zq, c // MEM_WIDTH, memkv, tm=min(tm, 512))
    row = lambda a: a.reshape(1, -1)
    return _outproj_ln(main, memo, h, w_out.astype(BF16), row(ln1_g), row(ln1_b), tm=min(tm, 256))


def kernel(x, mem, positions, a_w_in, a_mu, a_w0, a_w_decay_up, a_a0, a_w_a_up, a_w_g_up, a_k_k, a_k_a, a_r_k, a_gn_g, a_gn_b, kv_shared_w, b_w_in, b_lambda_q1, b_lambda_k1, b_lambda_q2, b_lambda_k2, b_subln_g, mem_w_kv, w_out, ln1_g, ln1_b, ln2_g, ln2_b, router_w, router_bias, exp_w1, exp_w3, exp_w2, sh_w1, sh_w3, sh_w2):
    bsz, t, _ = x.shape
    assert bsz == 1
    h = x.reshape(t, D_MODEL)
    tm = min(t, 1024)
    mem2 = mem.reshape(-1, D_MODEL)
    n_a = DEPTH // 2
    for l in range(DEPTH):
        memkv = _proj(mem2, mem_w_kv[l].astype(BF16), tm=mem2.shape[0], tn=512, out_dtype=F32)
        if l < n_a:
            h, h_packed = _layer_a(h, memkv, a_w_in[l], a_mu[l], a_w0[l], a_w_decay_up[l], a_a0[l], a_w_a_up[l],
                         a_w_g_up[l], a_k_k[l], a_k_a[l], a_r_k[l], a_gn_g[l], a_gn_b[l],
                         w_out[l], ln1_g[l], ln1_b[l], tm)
        else:
            j = l - n_a
            lam_init = 0.8 - 0.6 * math.exp(-0.3 * l)
            lam_params = jnp.stack([b_lambda_q1[j], b_lambda_k1[j], b_lambda_q2[j], b_lambda_k2[j]])
            h, h_packed = _layer_b(h, memkv, positions, kv_shared_w, b_w_in[j], lam_params, b_subln_g[j], lam_init,
                         w_out[l], ln1_g[l], ln1_b[l], tm)
        h = _moe(h, h_packed, router_w[l], router_bias[l], exp_w1, exp_w3, exp_w2, l,
                 sh_w1[l], sh_w3[l], sh_w2[l], ln2_g[l], ln2_b[l])
    return h.reshape(bsz, t, D_MODEL)
```
